```python
import jax, jax.numpy as jnp
from jax import lax
import numpy as np

D_MODEL = 1024
BATCH = 8
SEQ = 2048
DEPTH = 2
DEC_BATCH = 128
DEC_SEQ = 1
PAST_LEN = 16384
PAGE_SIZE = 128

MIX_WIDTH = D_MODEL // 4
N_BRANCH = 4
RWKV_HEAD = 64
RWKV_HEADS = MIX_WIDTH // RWKV_HEAD
RWKV_DECAY_LORA = 32
RWKV_AAA_LORA = 32
RWKV_GATE_LORA = 64
RWKV_PROJ = 3 * MIX_WIDTH + RWKV_DECAY_LORA + RWKV_AAA_LORA + RWKV_GATE_LORA
RWKV_GN_EPS = 64e-5
POOL_WINDOWS = (2, 4, 8, 16)
POOL_GROUPS = 4
POOL_GROUP_WIDTH = MIX_WIDTH // POOL_GROUPS
POOL_HIST = 15
CONV_KERNEL = 31
CONV_HIST = CONV_KERNEL - 1
CONV_LN_EPS = 1e-5
SSM_HEAD_DIM = 64
SSM_HEADS = MIX_WIDTH // SSM_HEAD_DIM
SSM_GROUPS = 2
SSM_STATE = 128
SSM_CONV_KERNEL = 4
SSM_CONV_HIST = SSM_CONV_KERNEL - 1
SSM_CHUNK = 128
SSM_CONV_DIM = MIX_WIDTH + 2 * SSM_GROUPS * SSM_STATE
SSM_PROJ = MIX_WIDTH + SSM_CONV_DIM + SSM_HEADS
OFF_POOL = RWKV_PROJ
OFF_CONV = OFF_POOL + MIX_WIDTH
OFF_SSM = OFF_CONV + 2 * MIX_WIDTH
OFF_GATE = OFF_SSM + SSM_PROJ
IN_PROJ = OFF_GATE + N_BRANCH * D_MODEL
MOE_GROUPS = 4
MOE_PER_GROUP = 4
MOE_EXPERTS = MOE_GROUPS * MOE_PER_GROUP
MOE_TOPK = 2
MOE_HIDDEN = 256
NORM_EPS = 1e-6

kernel_name = 'hybrid_rwkv7_pool_conformer_ssd_hmoe_step'


def rmsnorm(x, g):
    xf = x.astype(jnp.float32)
    y = xf * lax.rsqrt(jnp.mean(xf * xf, -1, keepdims=True) + NORM_EPS)
    return (y * g.astype(jnp.float32)).astype(x.dtype)


def layernorm(x, g, b, eps):
    xf = x.astype(jnp.float32)
    mu = jnp.mean(xf, -1, keepdims=True)
    var = jnp.mean(jnp.square(xf - mu), -1, keepdims=True)
    return ((xf - mu) * lax.rsqrt(var + eps) * g + b).astype(x.dtype)


def causal_dwconv(hist, u, w, b):
    L = u.shape[1]
    full = jnp.concatenate([hist.astype(u.dtype), u], axis=1)
    y = lax.conv_general_dilated(full, w.astype(u.dtype)[:, None, :], (1,), 'VALID',
                                 dimension_numbers=('NWC', 'WIO', 'NWC'), feature_group_count=u.shape[-1])
    return y + b.astype(u.dtype), full[:, L:]


def rwkv7_mix(p, shift0, wkv0, lw):
    bsz, L, _ = p.shape
    W, H, N = MIX_WIDTH, RWKV_HEADS, RWKV_HEAD
    f32 = jnp.float32
    prev = jnp.concatenate([shift0[:, None, :].astype(p.dtype), p[:, :-1]], axis=1)
    q = p + (prev - p) * lw['rwkv_mu']
    r, k, v = q[..., :W], q[..., W:2 * W], q[..., 2 * W:3 * W]
    o = 3 * W
    wl = q[..., o:o + RWKV_DECAY_LORA]
    o += RWKV_DECAY_LORA
    al = q[..., o:o + RWKV_AAA_LORA]
    o += RWKV_AAA_LORA
    gl = q[..., o:o + RWKV_GATE_LORA]
    w_log = -jax.nn.softplus(-(lw['rwkv_w0'] + jnp.tanh(wl) @ lw['rwkv_w2'])) - 0.5
    decay = jnp.exp(-jnp.exp(w_log.astype(f32)))
    a = jax.nn.sigmoid(lw['rwkv_a0'] + al @ lw['rwkv_a2'])
    g = jax.nn.sigmoid(gl) @ lw['rwkv_g2']
    heads = lambda t: t.astype(f32).reshape(bsz, L, H, N)
    kk = heads(k * lw['rwkv_kk'])
    kk = kk * lax.rsqrt(jnp.maximum(jnp.sum(kk * kk, -1, keepdims=True), 1e-24))
    k = k * (1.0 + (a - 1.0) * lw['rwkv_ka'])
    rh, kh, vh, ah, dh = heads(r), heads(k), heads(v), heads(a), heads(decay)
    bh = kk * ah

    def step(S, inp):
        r_t, d_t, k_t, v_t, kk_t, b_t = inp
        sa = -jnp.einsum('bhij,bhj->bhi', S, kk_t)
        S = S * d_t[:, :, None, :] + sa[..., None] * b_t[:, :, None, :] + v_t[..., None] * k_t[:, :, None, :]
        return S, jnp.einsum('bhij,bhj->bhi', S, r_t)

    tmaj = lambda t: jnp.swapaxes(t, 0, 1)
    S_last, y = lax.scan(step, wkv0.astype(f32), tuple(tmaj(t) for t in (rh, dh, kh, vh, kk, bh)))
    y = tmaj(y)
    mu = jnp.mean(y, -1, keepdims=True)
    var = jnp.mean(jnp.square(y - mu), -1, keepdims=True)
    y = (y - mu) * lax.rsqrt(var + RWKV_GN_EPS)
    y = y * lw['rwkv_ln_g'].reshape(H, N) + lw['rwkv_ln_b'].reshape(H, N)
    y = y + jnp.sum(rh * kh * lw['rwkv_rk'], -1, keepdims=True) * vh
    out = y.reshape(bsz, L, W) * g
    return out.astype(p.dtype), p[:, -1], S_last


def pool_mix(u, hist, pos0, lw):
    bsz, L, W = u.shape
    f32 = jnp.float32
    full = jnp.concatenate([hist.astype(u.dtype), u], axis=1)
    cs = jnp.concatenate([jnp.zeros((bsz, 1, W), f32), jnp.cumsum(full.astype(f32), axis=1)], axis=1)
    pos = pos0 + jnp.arange(L, dtype=jnp.int32)
    end = cs[:, POOL_HIST + 1:]
    means = []
    for gi, win in enumerate(POOL_WINDOWS):
        c = slice(gi * POOL_GROUP_WIDTH, (gi + 1) * POOL_GROUP_WIDTH)
        start = cs[:, POOL_HIST + 1 - win:POOL_HIST + 1 - win + L, c]
        count = jnp.minimum(pos + 1, win).astype(f32)[None, :, None]
        means.append((end[..., c] - start) / count)
    diff = (jnp.concatenate(means, -1) - u.astype(f32)).reshape(bsz, L, POOL_GROUPS, POOL_GROUP_WIDTH)
    out = jnp.einsum('blgc,gcd->blgd', diff, lw['pool_w'].astype(f32)).reshape(bsz, L, W) * lw['pool_scale']
    return out.astype(u.dtype), full[:, L:]


def conformer_conv(s, hist, lw):
    W = MIX_WIDTH
    c = s[..., :W] * jax.nn.sigmoid(s[..., W:])
    y, hist1 = causal_dwconv(hist, c, lw['conv_w'], lw['conv_b'])
    y = layernorm(y, lw['conv_ln_g'], lw['conv_ln_b'], CONV_LN_EPS)
    return jax.nn.silu(y), hist1


def ssd_chunked(x, dt, a, bm, cm, h0):
    f32 = jnp.float32
    bsz, L = x.shape[0], x.shape[1]
    rep = SSM_HEADS // SSM_GROUPS
    q = min(SSM_CHUNK, L)
    pad = (-L) % q
    xdt = x.astype(f32) * dt[..., None]
    da = dt * a
    bh = jnp.repeat(bm.astype(f32), rep, axis=2)
    ch = jnp.repeat(cm.astype(f32), rep, axis=2)
    if pad:
        padw = lambda t: jnp.pad(t, [(0, 0), (0, pad)] + [(0, 0)] * (t.ndim - 2))
        xdt, da, bh, ch = padw(xdt), padw(da), padw(bh), padw(ch)
    nc = (L + pad) // q
    chunk = lambda t: t.reshape((bsz, nc, q) + t.shape[2:])
    xdt, da, bh, ch = chunk(xdt), chunk(da), chunk(bh), chunk(ch)
    acs = jnp.cumsum(da, axis=2)
    causal = jnp.tril(jnp.ones((q, q), dtype=bool))[None, None, :, :, None]
    seg = acs[:, :, :, None, :] - acs[:, :, None, :, :]
    lmat = jnp.exp(jnp.where(causal, seg, -jnp.inf))
    scores = jnp.einsum('bcihn,bcjhn->bcijh', ch, bh) * lmat
    y_diag = jnp.einsum('bcijh,bcjhp->bcihp', scores, xdt)
    to_end = jnp.exp(acs[:, :, -1:, :] - acs)
    states = jnp.einsum('bcjhn,bcjh,bcjhp->bchpn', bh, to_end, xdt)
    chunk_decay = jnp.exp(acs[:, :, -1, :])

    def step(h, inp):
        st, dec = inp
        return h * dec[:, :, None, None] + st, h

    h_last, h_in = lax.scan(step, h0.astype(f32), (jnp.moveaxis(states, 1, 0), jnp.moveaxis(chunk_decay, 1, 0)))
    h_in = jnp.moveaxis(h_in, 0, 1)
    y_off = jnp.einsum('bcihn,bchpn,bcih->bcihp', ch, h_in, jnp.exp(acs))
    y = (y_diag + y_off).reshape((bsz, nc * q) + x.shape[2:])[:, :L]
    return y, h_last


def mamba2_mix(s, conv0, ssm0, lw):
    bsz, L, _ = s.shape
    W, H, P, G, N = MIX_WIDTH, SSM_HEADS, SSM_HEAD_DIM, SSM_GROUPS, SSM_STATE
    f32 = jnp.float32
    z = s[..., :W]
    xbc, conv1 = causal_dwconv(conv0, s[..., W:W + SSM_CONV_DIM], lw['ssm_conv_w'], lw['ssm_conv_b'])
    xbc = jax.nn.silu(xbc)
    dt = jax.nn.softplus(s[..., W + SSM_CONV_DIM:].astype(f32) + lw['ssm_dt_bias'].astype(f32))
    xs = xbc[..., :W].reshape(bsz, L, H, P)
    bm = xbc[..., W:W + G * N].reshape(bsz, L, G, N)
    cm = xbc[..., W + G * N:].reshape(bsz, L, G, N)
    a = -jnp.exp(lw['ssm_a_log'].astype(f32))
    y, ssm1 = ssd_chunked(xs, dt, a, bm, cm, ssm0)
    y = y + lw['ssm_d'].astype(f32)[:, None] * xs.astype(f32)
    y = y.reshape(bsz, L, W) * jax.nn.silu(z.astype(f32))
    return rmsnorm(y, lw['ssm_norm_g']).astype(s.dtype), conv1, ssm1


def token_mixers(h, st, pos0, lw):
    shift0, wkv0, pool0, conv0, sconv0, ssm0 = st
    bsz, L, _ = h.shape
    proj = h @ lw['w_in']
    o_rwkv, shift1, wkv1 = rwkv7_mix(proj[..., :OFF_POOL], shift0, wkv0, lw)
    o_pool, pool1 = pool_mix(proj[..., OFF_POOL:OFF_CONV], pool0, pos0, lw)
    o_conv, conv1 = conformer_conv(proj[..., OFF_CONV:OFF_SSM], conv0, lw)
    o_ssm, sconv1, ssm1 = mamba2_mix(proj[..., OFF_SSM:OFF_GATE], sconv0, ssm0, lw)
    gates = jax.nn.sigmoid(proj[..., OFF_GATE:]).reshape(bsz, L, N_BRANCH, D_MODEL)
    branches = jnp.stack([o_rwkv, o_pool, o_conv.astype(h.dtype), o_ssm], axis=2)
    up = jnp.einsum('blkc,kcd->blkd', branches, lw['w_branch'])
    merged = jnp.sum(up * gates, axis=2)
    return (merged @ lw['w_out']).astype(h.dtype), (shift1, wkv1, pool1, conv1, sconv1, ssm1)


def hier_moe(h, lw):
    f32 = jnp.float32
    g_logits = (h @ lw['moe_router_group_w'] + lw['moe_router_group_b']).astype(f32)
    g_prob = jax.nn.softmax(g_logits, -1)
    _, grp = lax.top_k(g_logits, 1)
    e_logits = (h @ lw['moe_router_expert_w'] + lw['moe_router_expert_b']).astype(f32)
    e_logits = e_logits.reshape(h.shape[:-1] + (MOE_GROUPS, MOE_PER_GROUP))
    in_group = jnp.take_along_axis(e_logits, grp[..., None], axis=-2)[..., 0, :]
    top_v, top_i = lax.top_k(in_group, MOE_TOPK)
    w_top = jax.nn.softmax(top_v, -1) * jnp.take_along_axis(g_prob, grp, -1)
    eid = grp * MOE_PER_GROUP + top_i
    gate = jnp.sum(jax.nn.one_hot(eid, MOE_EXPERTS, dtype=f32) * w_top[..., None], axis=-2)
    out = jnp.zeros(h.shape, f32)
    for gi in range(MOE_GROUPS):
        e = slice(gi * MOE_PER_GROUP, (gi + 1) * MOE_PER_GROUP)
        hg = jnp.einsum('bld,edf->blef', h, lw['moe_w_gate'][e])
        hu = jnp.einsum('bld,edf->blef', h, lw['moe_w_up'][e])
        act = jax.nn.silu(hg) * hu * gate[..., e, None].astype(h.dtype)
        out = out + jnp.einsum('blef,efd->bld', act, lw['moe_w_down'][e])
    return out.astype(h.dtype)


def run_trunk(x, states, pos0, params, final_norm_g):
    new = [[] for _ in range(len(states))]
    for l in range(DEPTH):
        lw = {name: arr[l] for name, arr in params.items()}
        st = tuple(s[l] for s in states)
        m, st1 = token_mixers(rmsnorm(x, lw['norm_mix_g']), st, pos0, lw)
        x = x + m
        x = x + hier_moe(rmsnorm(x, lw['norm_ffn_g']), lw)
        for lst, s in zip(new, st1):
            lst.append(s.astype(x.dtype))
    return rmsnorm(x, final_norm_g), tuple(jnp.stack(lst) for lst in new)


def setup_inputs(seed: int = 0) -> dict:
    key = jax.random.key(seed)
    ks = iter(jax.random.split(key, 64))
    f32 = jnp.float32
    nrm = lambda shape, scale: scale * jax.random.normal(next(ks), shape, f32)
    gain = lambda shape: 1.0 + 0.1 * jax.random.normal(next(ks), shape, f32)
    L, W, D = DEPTH, MIX_WIDTH, D_MODEL
    dt_init = jnp.exp(jax.random.uniform(next(ks), (L, SSM_HEADS), f32, np.log(1e-3), np.log(1e-1)))
    return {
        'x_prompt': nrm((BATCH, SEQ, D), 1.0),
        'x_sample': nrm((DEC_BATCH, DEC_SEQ, D), 1.0),
        'state_rwkv_shift': nrm((L, DEC_BATCH, RWKV_PROJ), 1.0),
        'state_rwkv_wkv': nrm((L, DEC_BATCH, RWKV_HEADS, RWKV_HEAD, RWKV_HEAD), 0.3),
        'state_pool': nrm((L, DEC_BATCH, POOL_HIST, W), 1.0),
        'state_conv': nrm((L, DEC_BATCH, CONV_HIST, W), 0.5),
        'state_ssm_conv': nrm((L, DEC_BATCH, SSM_CONV_HIST, SSM_CONV_DIM), 1.0),
        'state_ssm': nrm((L, DEC_BATCH, SSM_HEADS, SSM_HEAD_DIM, SSM_STATE), 0.1),
        'norm_mix_g': gain((L, D)),
        'w_in': nrm((L, D, IN_PROJ), D ** -0.5),
        'rwkv_mu': jax.random.uniform(next(ks), (L, RWKV_PROJ), f32),
        'rwkv_w0': nrm((L, W), 0.5) - 1.0,
        'rwkv_w2': nrm((L, RWKV_DECAY_LORA, W), RWKV_DECAY_LORA ** -0.5),
        'rwkv_a0': nrm((L, W), 0.5),
        'rwkv_a2': nrm((L, RWKV_AAA_LORA, W), RWKV_AAA_LORA ** -0.5),
        'rwkv_g2': nrm((L, RWKV_GATE_LORA, W), RWKV_GATE_LORA ** -0.5),
        'rwkv_kk': gain((L, W)),
        'rwkv_ka': gain((L, W)),
        'rwkv_rk': nrm((L, RWKV_HEADS, RWKV_HEAD), 0.1),
        'rwkv_ln_g': gain((L, W)),
        'rwkv_ln_b': nrm((L, W), 0.01),
        'pool_w': nrm((L, POOL_GROUPS, POOL_GROUP_WIDTH, POOL_GROUP_WIDTH), POOL_GROUP_WIDTH ** -0.5),
        'pool_scale': gain((L, W)),
        'conv_w': nrm((L, CONV_KERNEL, W), CONV_KERNEL ** -0.5),
        'conv_b': nrm((L, W), 0.01),
        'conv_ln_g': gain((L, W)),
        'conv_ln_b': nrm((L, W), 0.01),
        'ssm_conv_w': nrm((L, SSM_CONV_KERNEL, SSM_CONV_DIM), 0.5),
        'ssm_conv_b': nrm((L, SSM_CONV_DIM), 0.01),
        'ssm_dt_bias': dt_init + jnp.log(-jnp.expm1(-dt_init)),
        'ssm_a_log': jnp.log(jax.random.uniform(next(ks), (L, SSM_HEADS), f32, 1.0, 16.0)),
        'ssm_d': gain((L, SSM_HEADS)),
        'ssm_norm_g': gain((L, W)),
        'w_branch': nrm((L, N_BRANCH, W, D), W ** -0.5),
        'w_out': nrm((L, D, D), D ** -0.5),
        'norm_ffn_g': gain((L, D)),
        'moe_router_group_w': nrm((L, D, MOE_GROUPS), D ** -0.5),
        'moe_router_group_b': nrm((L, MOE_GROUPS), 0.01),
        'moe_router_expert_w': nrm((L, D, MOE_EXPERTS), D ** -0.5),
        'moe_router_expert_b': nrm((L, MOE_EXPERTS), 0.01),
        'moe_w_gate': nrm((L, MOE_EXPERTS, D, MOE_HIDDEN), D ** -0.5),
        'moe_w_up': nrm((L, MOE_EXPERTS, D, MOE_HIDDEN), D ** -0.5),
        'moe_w_down': nrm((L, MOE_EXPERTS, MOE_HIDDEN, D), MOE_HIDDEN ** -0.5),
        'final_norm_g': gain((D,)),
    }


def reference(x_prompt, x_sample, state_rwkv_shift, state_rwkv_wkv, state_pool, state_conv, state_ssm_conv, state_ssm,
              norm_mix_g, w_in, rwkv_mu, rwkv_w0, rwkv_w2, rwkv_a0, rwkv_a2, rwkv_g2, rwkv_kk, rwkv_ka, rwkv_rk,
              rwkv_ln_g, rwkv_ln_b, pool_w, pool_scale, conv_w, conv_b, conv_ln_g, conv_ln_b, ssm_conv_w, ssm_conv_b,
              ssm_dt_bias, ssm_a_log, ssm_d, ssm_norm_g, w_branch, w_out, norm_ffn_g, moe_router_group_w,
              moe_router_group_b, moe_router_expert_w, moe_router_expert_b, moe_w_gate, moe_w_up, moe_w_down,
              final_norm_g):
    params = {
        'norm_mix_g': norm_mix_g, 'w_in': w_in, 'rwkv_mu': rwkv_mu, 'rwkv_w0': rwkv_w0, 'rwkv_w2': rwkv_w2,
        'rwkv_a0': rwkv_a0, 'rwkv_a2': rwkv_a2, 'rwkv_g2': rwkv_g2, 'rwkv_kk': rwkv_kk, 'rwkv_ka': rwkv_ka,
        'rwkv_rk': rwkv_rk, 'rwkv_ln_g': rwkv_ln_g, 'rwkv_ln_b': rwkv_ln_b, 'pool_w': pool_w,
        'pool_scale': pool_scale, 'conv_w': conv_w, 'conv_b': conv_b, 'conv_ln_g': conv_ln_g,
        'conv_ln_b': conv_ln_b, 'ssm_conv_w': ssm_conv_w, 'ssm_conv_b': ssm_conv_b, 'ssm_dt_bias': ssm_dt_bias,
        'ssm_a_log': ssm_a_log, 'ssm_d': ssm_d, 'ssm_norm_g': ssm_norm_g, 'w_branch': w_branch, 'w_out': w_out,
        'norm_ffn_g': norm_ffn_g, 'moe_router_group_w': moe_router_group_w,
        'moe_router_group_b': moe_router_group_b, 'moe_router_expert_w': moe_router_expert_w,
        'moe_router_expert_b': moe_router_expert_b, 'moe_w_gate': moe_w_gate, 'moe_w_up': moe_w_up,
        'moe_w_down': moe_w_down,
    }
    bp, dtp = x_prompt.shape[0], x_prompt.dtype
    empty = (jnp.zeros((DEPTH, bp, RWKV_PROJ), dtp),
             jnp.zeros((DEPTH, bp, RWKV_HEADS, RWKV_HEAD, RWKV_HEAD), dtp),
             jnp.zeros((DEPTH, bp, POOL_HIST, MIX_WIDTH), dtp),
             jnp.zeros((DEPTH, bp, CONV_HIST, MIX_WIDTH), dtp),
             jnp.zeros((DEPTH, bp, SSM_CONV_HIST, SSM_CONV_DIM), dtp),
             jnp.zeros((DEPTH, bp, SSM_HEADS, SSM_HEAD_DIM, SSM_STATE), dtp))
    y_prompt, (p_shift, p_wkv, p_pool, p_conv, p_ssm_conv, p_ssm) = run_trunk(
        x_prompt, empty, 0, params, final_norm_g)
    past = (state_rwkv_shift, state_rwkv_wkv, state_pool, state_conv, state_ssm_conv, state_ssm)
    y_sample, (s_shift, s_wkv, s_pool, s_conv, s_ssm_conv, s_ssm) = run_trunk(
        x_sample, past, PAST_LEN, params, final_norm_g)
    return (y_prompt, y_sample, p_shift, p_wkv, p_pool, p_conv, p_ssm_conv, p_ssm,
            s_shift, s_wkv, s_pool, s_conv, s_ssm_conv, s_ssm)
```

```python
import functools
import math

import jax
import jax.numpy as jnp
from jax import lax
from jax.experimental import pallas as pl
from jax.experimental.pallas import tpu as pltpu

F32 = jnp.float32
BF16 = jnp.bfloat16
HIGHEST = lax.Precision.HIGHEST

D_MODEL = 1024
DEPTH = 2
PAST_LEN = 16384
MIX = D_MODEL // 4
N_BRANCH = 4
RWKV_HEAD = 64
RWKV_HEADS = MIX // RWKV_HEAD
RWKV_DECAY_LORA = 32
RWKV_AAA_LORA = 32
RWKV_GATE_LORA = 64
RWKV_PROJ = 3 * MIX + RWKV_DECAY_LORA + RWKV_AAA_LORA + RWKV_GATE_LORA
RWKV_GN_EPS = 64e-5
POOL_WINDOWS = (2, 4, 8, 16)
POOL_HIST = 15
CONV_KERNEL = 31
CONV_HIST = CONV_KERNEL - 1
CONV_LN_EPS = 1e-5
SSM_HEAD_DIM = 64
SSM_HEADS = MIX // SSM_HEAD_DIM
SSM_GROUPS = 2
SSM_STATE = 128
SSM_CONV_KERNEL = 4
SSM_CONV_HIST = SSM_CONV_KERNEL - 1
SSM_CONV_DIM = MIX + 2 * SSM_GROUPS * SSM_STATE
SSM_PROJ = MIX + SSM_CONV_DIM + SSM_HEADS
OFF_POOL = RWKV_PROJ
OFF_CONV = OFF_POOL + MIX
OFF_SSM = OFF_CONV + 2 * MIX
OFF_GATE = OFF_SSM + SSM_PROJ
MOE_GROUPS = 4
MOE_PER_GROUP = 4
MOE_EXPERTS = MOE_GROUPS * MOE_PER_GROUP
MOE_HIDDEN = 256
NORM_EPS = 1e-6

LANES = 128
SUBLANES = 8
VMEM_LIMIT_BYTES = 48 * 1024 * 1024

SSM_IN_PAD = 9 * LANES
MIX_IN_PAD = OFF_SSM + SSM_IN_PAD
ROUTER_PAD = LANES
SSD_CHUNK = 128
RWKV_CHUNK = 16
RWKV_BATCH_BLOCK = 8
NEG_BIG = -1e30


def _sigmoid(x):
    return 1.0 / (1.0 + jnp.exp(-x))


def _silu(x):
    return x * _sigmoid(x)


def _softplus(x):
    return jnp.maximum(x, 0.0) + jnp.log(1.0 + jnp.exp(-jnp.abs(x)))


def _dot(a, b):
    return jnp.dot(a, b, preferred_element_type=F32)


def _dot_hi(a, b):
    return jnp.dot(a, b, preferred_element_type=F32, precision=HIGHEST)


def _head_ones(n, head):
    r = lax.broadcasted_iota(jnp.int32, (n, n), 0) // head
    c = lax.broadcasted_iota(jnp.int32, (n, n), 1) // head
    return (r == c).astype(F32)


def _params(sem):
    return pltpu.CompilerParams(dimension_semantics=sem, vmem_limit_bytes=VMEM_LIMIT_BYTES)


def _const_spec(shape):
    nd = len(shape)
    return pl.BlockSpec(shape, lambda *_: (0,) * nd)


def _inproj_kernel(x_ref, g_ref, w_ref, h_ref, rw_ref, pool_ref, conv_ref, ssm_ref):
    x = x_ref[...]
    y = x * lax.rsqrt(jnp.mean(x * x, -1, keepdims=True) + NORM_EPS) * g_ref[...]
    hb = y.astype(BF16)
    h_ref[...] = hb
    rw_ref[...] = _dot(hb, w_ref[:, 0:OFF_POOL])
    pool_ref[...] = _dot(hb, w_ref[:, OFF_POOL:OFF_CONV])
    conv_ref[...] = _dot(hb, w_ref[:, OFF_CONV:OFF_SSM])
    ssm_ref[...] = _dot(hb, w_ref[:, OFF_SSM:MIX_IN_PAD])


def _inproj(x2, g, w_mix):
    t = x2.shape[0]
    tm = min(t, 512)
    row = lambda w: pl.BlockSpec((tm, w), lambda i: (i, 0))
    return pl.pallas_call(
        _inproj_kernel,
        grid=(t // tm,),
        in_specs=[row(D_MODEL), _const_spec((1, D_MODEL)), _const_spec((D_MODEL, MIX_IN_PAD))],
        out_specs=[row(D_MODEL), row(RWKV_PROJ), row(MIX), row(2 * MIX), row(SSM_IN_PAD)],
        out_shape=[jax.ShapeDtypeStruct((t, D_MODEL), BF16),
                   jax.ShapeDtypeStruct((t, RWKV_PROJ), F32),
                   jax.ShapeDtypeStruct((t, MIX), F32),
                   jax.ShapeDtypeStruct((t, 2 * MIX), F32),
                   jax.ShapeDtypeStruct((t, SSM_IN_PAD), F32)],
        compiler_params=_params(("parallel",)),
        name="inproj",
    )(x2, g, w_mix)


def _rwkv_prep_math(p, prev, mu, w0, a0, kkw, kaw, w2p, a2p, g2p):
    q = p + (prev - p) * mu
    r = q[:, 0:MIX]
    k = q[:, MIX:2 * MIX]
    v = q[:, 2 * MIX:3 * MIX]
    lora = q[:, 3 * MIX:RWKV_PROJ]
    zw = w0 + _dot_hi(jnp.tanh(lora), w2p)
    decay = jnp.exp(-math.exp(-0.5) * _sigmoid(zw))
    a = _sigmoid(a0 + _dot_hi(lora, a2p))
    g = _dot_hi(_sigmoid(lora), g2p)
    kk = k * kkw
    ss = _dot_hi(kk * kk, _head_ones(MIX, RWKV_HEAD))
    kk = kk * lax.rsqrt(jnp.maximum(ss, 1e-24))
    k2 = k * (1.0 + (a - 1.0) * kaw)
    return r, k2, v, kk, kk * a, decay, g


def _rwkv_prep_seq_kernel(p_ref, s0_ref, mu_ref, w0_ref, a0_ref, kkw_ref, kaw_ref, w2_ref, a2_ref, g2_ref,
                          r_ref, k_ref, v_ref, kk_ref, b_ref, d_ref, g_ref, carry_ref):
    @pl.when(pl.program_id(1) == 0)
    def _():
        carry_ref[...] = s0_ref[...]

    p = p_ref[...]
    rows = p.shape[0]
    first = lax.broadcasted_iota(jnp.int32, p.shape, 0) == 0
    prev = jnp.where(first, carry_ref[...], pltpu.roll(p, 1, 0))
    carry_ref[...] = p[rows - 1:rows, :]
    outs = _rwkv_prep_math(p, prev, mu_ref[...], w0_ref[...], a0_ref[...], kkw_ref[...], kaw_ref[...],
                           w2_ref[...], a2_ref[...], g2_ref[...])
    for ref, val in zip((r_ref, k_ref, v_ref, kk_ref, b_ref, d_ref, g_ref), outs):
        ref[...] = val


def _rwkv_prep_tok_kernel(p_ref, prev_ref, mu_ref, w0_ref, a0_ref, kkw_ref, kaw_ref, w2_ref, a2_ref, g2_ref,
                          r_ref, k_ref, v_ref, kk_ref, b_ref, d_ref, g_ref):
    outs = _rwkv_prep_math(p_ref[...], prev_ref[...], mu_ref[...], w0_ref[...], a0_ref[...], kkw_ref[...],
                           kaw_ref[...], w2_ref[...], a2_ref[...], g2_ref[...])
    for ref, val in zip((r_ref, k_ref, v_ref, kk_ref, b_ref, d_ref, g_ref), outs):
        ref[...] = val


def _rwkv_prep(p3, shift0, lw):
    b, l, _ = p3.shape
    consts = [lw['rwkv_mu'], lw['rwkv_w0'], lw['rwkv_a0'], lw['rwkv_kk'], lw['rwkv_ka'],
              lw['rwkv_w2p'], lw['rwkv_a2p'], lw['rwkv_g2p']]
    const_specs = [_const_spec(c.shape) for c in consts]
    if l == 1:
        outs = pl.pallas_call(
            _rwkv_prep_tok_kernel,
            grid=(1,),
            in_specs=[_const_spec((b, RWKV_PROJ)), _const_spec((b, RWKV_PROJ))] + const_specs,
            out_specs=[_const_spec((b, MIX))] * 7,
            out_shape=[jax.ShapeDtypeStruct((b, MIX), F32)] * 7,
            compiler_params=_params(("arbitrary",)),
            name="rwkv_prep_tok",
        )(p3[:, 0], shift0, *consts)
        return [o[:, None, :] for o in outs]
    lt = min(l, 512)
    seq = lambda w: pl.BlockSpec((None, lt, w), lambda i, j: (i, j, 0))
    return pl.pallas_call(
        _rwkv_prep_seq_kernel,
        grid=(b, l // lt),
        in_specs=[seq(RWKV_PROJ), pl.BlockSpec((None, 1, RWKV_PROJ), lambda i, j: (i, 0, 0))] + const_specs,
        out_specs=[seq(MIX)] * 7,
        out_shape=[jax.ShapeDtypeStruct((b, l, MIX), F32)] * 7,
        scratch_shapes=[pltpu.VMEM((1, RWKV_PROJ), F32)],
        compiler_params=_params(("parallel", "arbitrary")),
        name="rwkv_prep_seq",
    )(p3, shift0[:, None, :], *consts)


def _rwkv_scan_kernel(r_ref, k_ref, v_ref, kk_ref, b_ref, d_ref, g_ref, lng_ref, lnb_ref, rk_ref, s0_ref,
                      o_ref, s1_ref, st_ref, col_ref, row_ref, y_ref, *, nb, lc):
    c = pl.program_id(2)
    n = RWKV_HEAD

    @pl.when(c == 0)
    def _():
        st_ref[...] = s0_ref[...]

    ones = _head_ones(LANES, n)
    lane = lax.broadcasted_iota(jnp.int32, (n, LANES), 1)
    sub = lax.broadcasted_iota(jnp.int32, (n, LANES), 0)
    diag = (lane % n == sub).astype(F32)

    for bb in range(nb):
        r = r_ref[bb]
        k = k_ref[bb]
        kk = kk_ref[bb]
        bv = b_ref[bb]
        d = d_ref[bb]
        for idx, x in enumerate((kk, d, bv, k, d * r)):
            spread = (x[:, None, :] * diag[None, :, :]).reshape(lc * n, LANES)
            col_ref[idx, bb] = _dot_hi(spread, ones)
        row_ref[0, bb] = _dot_hi(bv * r, ones)
        row_ref[1, bb] = _dot_hi(k * r, ones)

    def step(t, carry):
        base = pl.multiple_of(t * n, n)
        for bb in range(nb):
            st = st_ref[bb]
            ckk = col_ref[0, bb, pl.ds(base, n), :]
            cd = col_ref[1, bb, pl.ds(base, n), :]
            cb = col_ref[2, bb, pl.ds(base, n), :]
            ck = col_ref[3, bb, pl.ds(base, n), :]
            cdr = col_ref[4, bb, pl.ds(base, n), :]
            vrow = v_ref[bb, pl.ds(t, 1), :]
            br = row_ref[0, bb, pl.ds(t, 1), :]
            kr = row_ref[1, bb, pl.ds(t, 1), :]
            sa = -jnp.sum(st * ckk, axis=0, keepdims=True)
            y = jnp.sum(st * cdr, axis=0, keepdims=True) + sa * br + vrow * kr
            st_ref[bb] = st * cd + sa * cb + vrow * ck
            y_ref[bb, pl.ds(t, 1), :] = y
        return carry

    lax.fori_loop(0, lc, step, 0)

    lng = lng_ref[...]
    lnb = lnb_ref[...]
    rk = rk_ref[...]
    inv_n = 1.0 / n
    for bb in range(nb):
        y = y_ref[bb]
        mu = _dot_hi(y, ones) * inv_n
        yc = y - mu
        var = _dot_hi(yc * yc, ones) * inv_n
        yn = yc * lax.rsqrt(var + RWKV_GN_EPS) * lng + lnb
        r = r_ref[bb]
        v = v_ref[bb]
        rkv = _dot_hi(r * k_ref[bb] * rk, ones)
        o_ref[bb] = (yn + rkv * v) * g_ref[bb]

    @pl.when(c == pl.num_programs(2) - 1)
    def _():
        s1_ref[...] = st_ref[...]


def _rwkv_scan(prep, lw, s0t):
    r, k, v, kk, bv, d, g = prep
    b, l, _ = r.shape
    nb = RWKV_BATCH_BLOCK
    lc = min(l, RWKV_CHUNK)
    n = RWKV_HEAD
    seq = pl.BlockSpec((nb, lc, LANES), lambda i, p, c: (i, c, p))
    vec = pl.BlockSpec((1, LANES), lambda i, p, c: (0, p))
    st = pl.BlockSpec((nb, None, n, LANES), lambda i, p, c: (i, p, 0, 0))
    kern = functools.partial(_rwkv_scan_kernel, nb=nb, lc=lc)
    return pl.pallas_call(
        kern,
        grid=(b // nb, 2, l // lc),
        in_specs=[seq] * 7 + [vec] * 3 + [st],
        out_specs=[seq, st],
        out_shape=[jax.ShapeDtypeStruct((b, l, MIX), F32), jax.ShapeDtypeStruct((b, 2, n, LANES), F32)],
        scratch_shapes=[pltpu.VMEM((nb, n, LANES), F32),
                        pltpu.VMEM((5, nb, lc * n, LANES), F32),
                        pltpu.VMEM((2, nb, lc, LANES), F32),
                        pltpu.VMEM((nb, lc, LANES), F32)],
        compiler_params=_params(("parallel", "parallel", "arbitrary")),
        name="rwkv_scan",
    )(r, k, v, kk, bv, d, g, lw['rwkv_ln_g'], lw['rwkv_ln_b'], lw['rwkv_rk'], s0t)


def _wkv_to_pairs(s):
    b = s.shape[0]
    n = RWKV_HEAD
    return s.reshape(b, 2, 2, n, n).transpose(0, 1, 4, 2, 3).reshape(b, 2, n, 2 * n)


def _wkv_from_pairs(s):
    b = s.shape[0]
    n = RWKV_HEAD
    return s.reshape(b, 2, n, 2, n).transpose(0, 1, 3, 4, 2).reshape(b, RWKV_HEADS, n, n)


def _pool_math(tap, pos, pw, scale):
    rows = pos.shape[0]
    lane = lax.broadcasted_iota(jnp.int32, (rows, LANES), 1)
    low = lane < (LANES // 2)
    cnt = lambda w: jnp.minimum(pos + 1, w).astype(F32)
    a0 = tap(0, 0)
    s2 = a0 + tap(1, 0)
    s4 = s2 + tap(2, 0) + tap(3, 0)
    b0 = tap(0, 1)
    s8 = b0
    for k in range(1, 8):
        s8 = s8 + tap(k, 1)
    s16 = s8
    for k in range(8, 16):
        s16 = s16 + tap(k, 1)
    diff_a = jnp.where(low, s2 / cnt(2), s4 / cnt(4)) - a0
    diff_b = jnp.where(low, s8 / cnt(8), s16 / cnt(16)) - b0
    diff = jnp.concatenate([diff_a, diff_b], axis=-1).astype(BF16)
    return _dot(diff, pw) * scale


def _pool_seq_kernel(u_ref, h_ref, pw_ref, sc_ref, o_ref, h1_ref, full_ref, *, l, tt):
    pad = POOL_HIST + 1
    full_ref[0:1, :] = jnp.zeros((1, MIX), F32)
    full_ref[1:pad, :] = h_ref[...]
    full_ref[pad:pad + l, :] = u_ref[...]
    pw = pw_ref[...]
    sc = sc_ref[...]
    for i in range(l // tt):
        t0 = i * tt
        tap = lambda k, half: full_ref[pad + t0 - k:pad + t0 - k + tt, half * LANES:(half + 1) * LANES]
        pos = t0 + lax.broadcasted_iota(jnp.int32, (tt, 1), 0)
        o_ref[t0:t0 + tt, :] = _pool_math(tap, pos, pw, sc)
    h1_ref[...] = full_ref[l + 1:l + pad, :]


def _pool_tok_kernel(u_ref, ht_ref, pw_ref, sc_ref, o_ref, *, pos0):
    def tap(k, half):
        sl = slice(half * LANES, (half + 1) * LANES)
        return u_ref[:, sl] if k == 0 else ht_ref[POOL_HIST - k, :, sl]
    rows = u_ref.shape[0]
    pos = jnp.full((rows, 1), pos0, jnp.int32)
    o_ref[...] = _pool_math(tap, pos, pw_ref[...], sc_ref[...])


def _pool(u3, hist, pos0, lw):
    b, l, _ = u3.shape
    consts = [lw['pool_wbd'], lw['pool_scale']]
    const_specs = [_const_spec(c.shape) for c in consts]
    if l == 1:
        ht = jnp.swapaxes(hist, 0, 1)
        out = pl.pallas_call(
            functools.partial(_pool_tok_kernel, pos0=pos0),
            grid=(1,),
            in_specs=[_const_spec((b, MIX)), _const_spec((POOL_HIST, b, MIX))] + const_specs,
            out_specs=_const_spec((b, MIX)),
            out_shape=jax.ShapeDtypeStruct((b, MIX), F32),
            compiler_params=_params(("arbitrary",)),
            name="pool_tok",
        )(u3[:, 0], ht, *consts)
        return out[:, None, :], jnp.concatenate([hist[:, 1:], u3], axis=1)
    assert pos0 == 0 and l >= POOL_HIST
    tt = min(l, 256)
    seq = pl.BlockSpec((None, l, MIX), lambda i: (i, 0, 0))
    hs = pl.BlockSpec((None, POOL_HIST, MIX), lambda i: (i, 0, 0))
    return pl.pallas_call(
        functools.partial(_pool_seq_kernel, l=l, tt=tt),
        grid=(b,),
        in_specs=[seq, hs] + const_specs,
        out_specs=[seq, hs],
        out_shape=[jax.ShapeDtypeStruct((b, l, MIX), F32), jax.ShapeDtypeStruct((b, POOL_HIST, MIX), F32)],
        scratch_shapes=[pltpu.VMEM((l + POOL_HIST + 1, MIX), F32)],
        compiler_params=_params(("parallel",)),
        name="pool_seq",
    )(u3, hist, *consts)


def _conv_post(y, cb, g, beta):
    y = y + cb
    mu = jnp.mean(y, -1, keepdims=True)
    yc = y - mu
    var = jnp.mean(yc * yc, -1, keepdims=True)
    return _silu(yc * lax.rsqrt(var + CONV_LN_EPS) * g + beta)


def _glu(s):
    return s[:, 0:MIX] * _sigmoid(s[:, MIX:2 * MIX])


def _conv_seq_kernel(s_ref, h_ref, w_ref, cb_ref, g_ref, beta_ref, o_ref, h1_ref, full_ref, *, l, tt):
    pad = CONV_HIST + 2
    full_ref[0:2, :] = jnp.zeros((2, MIX), F32)
    full_ref[2:pad, :] = h_ref[...]
    full_ref[pad:pad + l, :] = _glu(s_ref[...])
    cb = cb_ref[...]
    g = g_ref[...]
    beta = beta_ref[...]

    for i in range(l // tt):
        t0 = i * tt
        acc = jnp.zeros((tt, MIX), F32)
        for k in range(CONV_KERNEL):
            acc = acc + w_ref[k:k + 1, :] * full_ref[t0 + k + 2:t0 + k + 2 + tt, :]
        o_ref[t0:t0 + tt, :] = _conv_post(acc, cb, g, beta)
    h1_ref[...] = full_ref[l + 2:l + pad, :]


def _conv_tok_kernel(s_ref, ht_ref, w_ref, cb_ref, g_ref, beta_ref, o_ref, c_ref):
    c = _glu(s_ref[...])
    c_ref[...] = c
    acc = w_ref[CONV_HIST:CONV_KERNEL, :] * c
    for k in range(CONV_HIST):
        acc = acc + w_ref[k:k + 1, :] * ht_ref[k]
    o_ref[...] = _conv_post(acc, cb_ref[...], g_ref[...], beta_ref[...])


def _conv(s3, hist, lw):
    b, l, _ = s3.shape
    consts = [lw['conv_w'], lw['conv_b'], lw['conv_ln_g'], lw['conv_ln_b']]
    const_specs = [_const_spec(c.shape) for c in consts]
    if l == 1:
        ht = jnp.swapaxes(hist, 0, 1)
        out, c = pl.pallas_call(
            _conv_tok_kernel,
            grid=(1,),
            in_specs=[_const_spec((b, 2 * MIX)), _const_spec((CONV_HIST, b, MIX))] + const_specs,
            out_specs=[_const_spec((b, MIX))] * 2,
            out_shape=[jax.ShapeDtypeStruct((b, MIX), F32)] * 2,
            compiler_params=_params(("arbitrary",)),
            name="conv_tok",
        )(s3[:, 0], ht, *consts)
        return out[:, None, :], jnp.concatenate([hist[:, 1:], c[:, None, :]], axis=1)
    assert l >= CONV_HIST
    tt = min(l, 64)
    hs = pl.BlockSpec((None, CONV_HIST, MIX), lambda i: (i, 0, 0))
    return pl.pallas_call(
        functools.partial(_conv_seq_kernel, l=l, tt=tt),
        grid=(b,),
        in_specs=[pl.BlockSpec((None, l, 2 * MIX), lambda i: (i, 0, 0)), hs] + const_specs,
        out_specs=[pl.BlockSpec((None, l, MIX), lambda i: (i, 0, 0)), hs],
        out_shape=[jax.ShapeDtypeStruct((b, l, MIX), F32), jax.ShapeDtypeStruct((b, CONV_HIST, MIX), F32)],
        scratch_shapes=[pltpu.VMEM((l + CONV_HIST + 2, MIX), F32)],
        compiler_params=_params(("parallel",)),
        name="conv_seq",
    )(s3, hist, *consts)


def _ssd_kernel(zx_ref, ch_ref, h0_ref, cw_ref, cb_ref, dtb_ref, alog_ref, dd_ref, ng_ref,
                o_ref, h1_ref, xf_ref, ht_ref, *, lb, lvalid):
    q = SSD_CHUNK
    c = pl.program_id(1)
    halo = SUBLANES
    gw = SSM_GROUPS * SSM_STATE

    @pl.when(c == 0)
    def _():
        ht_ref[...] = h0_ref[...]
        xf_ref[...] = jnp.zeros(xf_ref.shape, F32)
        xf_ref[halo - SSM_CONV_HIST:halo, :] = ch_ref[...]

    xf_ref[halo:halo + lb, :] = zx_ref[:, MIX:MIX + SSM_CONV_DIM]
    acc = jnp.zeros((q, SSM_CONV_DIM), F32)
    for k in range(SSM_CONV_KERNEL):
        acc = acc + cw_ref[k:k + 1, :] * xf_ref[halo - SSM_CONV_HIST + k:halo - SSM_CONV_HIST + k + q, :]
    xbc = _silu(acc + cb_ref[...])
    xf_ref[halo - SSM_CONV_HIST:halo, :] = xf_ref[halo + q - SSM_CONV_HIST:halo + q, :]

    xs = xbc[:, 0:MIX]
    bm = xbc[:, MIX:MIX + gw]
    cm = xbc[:, MIX + gw:MIX + 2 * gw]

    hsel = (lax.broadcasted_iota(jnp.int32, (LANES, MIX), 0)
            == lax.broadcasted_iota(jnp.int32, (LANES, MIX), 1) // SSM_HEAD_DIM).astype(F32)
    dt_raw = zx_ref[:, MIX + SSM_CONV_DIM:SSM_IN_PAD]
    if lb < q:
        dt_raw = jnp.concatenate([dt_raw, jnp.zeros((q - lb, LANES), F32)], axis=0)
    trow = lax.broadcasted_iota(jnp.int32, (q, 1), 0) + c * q
    dt = jnp.where(trow < lvalid, _softplus(dt_raw + dtb_ref[...]), 0.0)
    dte = _dot_hi(dt, hsel)
    dae = dte * (-jnp.exp(alog_ref[...]))
    ti = lax.broadcasted_iota(jnp.int32, (q, q), 0)
    tj = lax.broadcasted_iota(jnp.int32, (q, q), 1)
    causal = ti >= tj
    acse = _dot_hi(causal.astype(F32), dae)
    xdt = xs * dte
    last = acse[q - 1:q, :]
    w_end = xdt * jnp.exp(last - acse)
    eacs = jnp.exp(acse)

    lane_g = lax.broadcasted_iota(jnp.int32, (q, LANES), 1)
    heads_per_group = SSM_HEADS // SSM_GROUPS
    ys = []
    for g in range(SSM_GROUPS):
        lo = g * LANES
        bgt = jnp.transpose(bm[:, g * SSM_STATE:(g + 1) * SSM_STATE]).astype(BF16)
        cg = cm[:, g * SSM_STATE:(g + 1) * SSM_STATE].astype(BF16)
        gmat = _dot(cg, bgt)
        hg = ht_ref[:, lo:lo + LANES]
        yg = _dot(cg, hg.astype(BF16)) * eacs[:, lo:lo + LANES]
        xg = xdt[:, lo:lo + LANES]
        for hh in range(heads_per_group):
            col = acse[:, lo + hh * SSM_HEAD_DIM:lo + hh * SSM_HEAD_DIM + 1]
            row = jnp.transpose(jnp.broadcast_to(col, (q, q)))
            seg = jnp.where(causal, col - row, NEG_BIG)
            scores = (gmat * jnp.exp(seg)).astype(BF16)
            xm = jnp.where(lane_g // SSM_HEAD_DIM == hh, xg, 0.0).astype(BF16)
            yg = yg + _dot(scores, xm)
        ys.append(yg)
        st = _dot(bgt, w_end[:, lo:lo + LANES].astype(BF16))
        ht_ref[:, lo:lo + LANES] = hg * jnp.exp(last[:, lo:lo + LANES]) + st
    y = jnp.concatenate(ys, axis=-1)

    y = y + dd_ref[...] * xs
    z = zx_ref[:, 0:MIX]
    if lb < q:
        z = jnp.concatenate([z, jnp.zeros((q - lb, MIX), F32)], axis=0)
    y = y * _silu(z)
    y = y * lax.rsqrt(jnp.mean(y * y, -1, keepdims=True) + NORM_EPS) * ng_ref[...]
    o_ref[...] = y[0:lb, :]

    @pl.when(c == pl.num_programs(1) - 1)
    def _():
        h1_ref[...] = ht_ref[...]


def _ssd(zx3, conv_hist, h0t, lvalid, lw):
    b, lp, _ = zx3.shape
    q = SSD_CHUNK
    lb = min(lp, q)
    consts = [lw['ssm_conv_w'], lw['ssm_conv_b'], lw['ssm_dt_bias_pad'], lw['ssm_a_log_e'], lw['ssm_d_e'],
              lw['ssm_norm_g']]
    const_specs = [pl.BlockSpec(cst.shape, lambda i, c: (0, 0)) for cst in consts]
    hs = pl.BlockSpec((None, SSM_STATE, MIX), lambda i, c: (i, 0, 0))
    return pl.pallas_call(
        functools.partial(_ssd_kernel, lb=lb, lvalid=lvalid),
        grid=(b, lp // lb),
        in_specs=[pl.BlockSpec((None, lb, SSM_IN_PAD), lambda i, c: (i, c, 0)),
                  pl.BlockSpec((None, SSM_CONV_HIST, SSM_CONV_DIM), lambda i, c: (i, 0, 0)), hs] + const_specs,
        out_specs=[pl.BlockSpec((None, lb, MIX), lambda i, c: (i, c, 0)), hs],
        out_shape=[jax.ShapeDtypeStruct((b, lp, MIX), F32), jax.ShapeDtypeStruct((b, SSM_STATE, MIX), F32)],
        scratch_shapes=[pltpu.VMEM((SUBLANES + q, SSM_CONV_DIM), F32), pltpu.VMEM((SSM_STATE, MIX), F32)],
        compiler_params=_params(("parallel", "arbitrary")),
        name="ssd",
    )(zx3, conv_hist, h0t, *consts)


def _ssm_to_t(s):
    b = s.shape[0]
    return s.transpose(0, 3, 1, 2).reshape(b, SSM_STATE, MIX)


def _ssm_from_t(s):
    b = s.shape[0]
    return s.reshape(b, SSM_STATE, SSM_HEADS, SSM_HEAD_DIM).transpose(0, 2, 3, 1)


def _merge_kernel(x_ref, h_ref, b0_ref, b1_ref, b2_ref, b3_ref, wg_ref, wb_ref, wo_ref, ng_ref, wr_ref, br_ref,
                  x1_ref, h2_ref, lg_ref):
    h = h_ref[...]
    merged = None
    for kbr, b_ref in enumerate((b0_ref, b1_ref, b2_ref, b3_ref)):
        gate = _sigmoid(_dot(h, wg_ref[:, kbr * D_MODEL:(kbr + 1) * D_MODEL]))
        up = _dot(b_ref[...].astype(BF16), wb_ref[kbr])
        merged = up * gate if merged is None else merged + up * gate
    x1 = x_ref[...] + _dot(merged.astype(BF16), wo_ref[...])
    x1_ref[...] = x1
    hn = x1 * lax.rsqrt(jnp.mean(x1 * x1, -1, keepdims=True) + NORM_EPS) * ng_ref[...]
    h2_ref[...] = hn.astype(BF16)
    lg_ref[...] = _dot_hi(hn, wr_ref[...]) + br_ref[...]


def _merge(x2, h2d, branches, lw):
    t = x2.shape[0]
    tm = min(t, 256)
    row = lambda w: pl.BlockSpec((tm, w), lambda i: (i, 0))
    consts = [lw['w_gate'], lw['w_branch'], lw['w_out'], lw['norm_ffn_g'], lw['router_w'], lw['router_b']]
    return pl.pallas_call(
        _merge_kernel,
        grid=(t // tm,),
        in_specs=[row(D_MODEL), row(D_MODEL)] + [row(MIX)] * 4 + [_const_spec(c.shape) for c in consts],
        out_specs=[row(D_MODEL), row(D_MODEL), row(ROUTER_PAD)],
        out_shape=[jax.ShapeDtypeStruct((t, D_MODEL), F32), jax.ShapeDtypeStruct((t, D_MODEL), BF16),
                   jax.ShapeDtypeStruct((t, ROUTER_PAD), F32)],
        compiler_params=_params(("parallel",)),
        name="merge",
    )(x2, h2d, *branches, *consts)


def _route(logits):
    lane = lax.broadcasted_iota(jnp.int32, logits.shape, 1).astype(F32)
    far = float(4 * LANES)
    is_group = lane < MOE_GROUPS
    gl = jnp.where(is_group, logits, NEG_BIG)
    gmax = jnp.max(gl, -1, keepdims=True)
    gidx = jnp.min(jnp.where(is_group & (gl == gmax), lane, far), -1, keepdims=True)
    gsum = jnp.sum(jnp.where(is_group, jnp.exp(gl - gmax), 0.0), -1, keepdims=True)
    first = MOE_GROUPS + MOE_PER_GROUP * gidx
    in_group = (lane >= first) & (lane < first + MOE_PER_GROUP)
    el = jnp.where(in_group, logits, NEG_BIG)
    m1 = jnp.max(el, -1, keepdims=True)
    i1 = jnp.min(jnp.where(in_group & (el == m1), lane, far), -1, keepdims=True)
    rest = in_group & (lane != i1)
    el2 = jnp.where(rest, logits, NEG_BIG)
    m2 = jnp.max(el2, -1, keepdims=True)
    i2 = jnp.min(jnp.where(rest & (el2 == m2), lane, far), -1, keepdims=True)
    e2 = jnp.exp(m2 - m1)
    w1 = 1.0 / (1.0 + e2)
    w2 = e2 / (1.0 + e2)
    return (jnp.where(lane == i1, w1, 0.0) + jnp.where(lane == i2, w2, 0.0)) / gsum


def _moe_kernel(h_ref, lg_ref, x1_ref, wgu_ref, wd_ref, fg_ref, o_ref, gate_ref, acc_ref, *, final_norm):
    e = pl.program_id(1)

    @pl.when(e == 0)
    def _():
        gate_ref[...] = _route(lg_ref[...])
        acc_ref[...] = jnp.zeros(acc_ref.shape, F32)

    lane = lax.broadcasted_iota(jnp.int32, gate_ref.shape, 1)
    gcol = jnp.sum(jnp.where(lane == e + MOE_GROUPS, gate_ref[...], 0.0), -1, keepdims=True)
    hgu = _dot(h_ref[...], wgu_ref[...])
    act = _silu(hgu[:, 0:MOE_HIDDEN]) * hgu[:, MOE_HIDDEN:2 * MOE_HIDDEN] * gcol
    acc_ref[...] += _dot(act.astype(BF16), wd_ref[...])

    @pl.when(e == pl.num_programs(1) - 1)
    def _():
        x2 = x1_ref[...] + acc_ref[...]
        if final_norm:
            x2 = x2 * lax.rsqrt(jnp.mean(x2 * x2, -1, keepdims=True) + NORM_EPS) * fg_ref[...]
        o_ref[...] = x2


def _moe(h2d, logits, x1, lw, final_g, final_norm):
    t = x1.shape[0]
    tm = min(t, 512)
    row = lambda w: pl.BlockSpec((tm, w), lambda i, e: (i, 0))
    return pl.pallas_call(
        functools.partial(_moe_kernel, final_norm=final_norm),
        grid=(t // tm, MOE_EXPERTS),
        in_specs=[row(D_MODEL), row(ROUTER_PAD), row(D_MODEL),
                  pl.BlockSpec((None, D_MODEL, 2 * MOE_HIDDEN), lambda i, e: (e, 0, 0)),
                  pl.BlockSpec((None, MOE_HIDDEN, D_MODEL), lambda i, e: (e, 0, 0)),
                  pl.BlockSpec((1, D_MODEL), lambda i, e: (0, 0))],
        out_specs=row(D_MODEL),
        out_shape=jax.ShapeDtypeStruct((t, D_MODEL), F32),
        scratch_shapes=[pltpu.VMEM((tm, ROUTER_PAD), F32), pltpu.VMEM((tm, D_MODEL), F32)],
        compiler_params=_params(("parallel", "arbitrary")),
        name="moe",
    )(h2d, logits, x1, lw['moe_w_gu'], lw['moe_w_down'], final_g)


def _prep_layer(params, l):
    g = lambda name: params[name][l]
    row = lambda a: a.reshape(1, -1).astype(F32)
    w_in = g('w_in')
    lora_rows = RWKV_PROJ - 3 * MIX

    def lora_pad(w, start):
        return jnp.zeros((lora_rows, MIX), F32).at[start:start + w.shape[0]].set(w)

    pw = g('pool_w')
    pool_wbd = jnp.zeros((MIX, MIX), F32)
    gw = MIX // len(POOL_WINDOWS)
    for gi in range(len(POOL_WINDOWS)):
        pool_wbd = pool_wbd.at[gi * gw:(gi + 1) * gw, gi * gw:(gi + 1) * gw].set(pw[gi])
    router_w = jnp.concatenate([g('moe_router_group_w'), g('moe_router_expert_w')], axis=1)
    router_b = jnp.concatenate([g('moe_router_group_b'), g('moe_router_expert_b')])
    n_r = router_w.shape[1]
    return {
        'norm_mix_g': row(g('norm_mix_g')),
        'w_mix': jnp.pad(w_in[:, :OFF_GATE], ((0, 0), (0, MIX_IN_PAD - OFF_GATE))).astype(BF16),
        'w_gate': w_in[:, OFF_GATE:].astype(BF16),
        'rwkv_mu': row(g('rwkv_mu')), 'rwkv_w0': row(g('rwkv_w0')), 'rwkv_a0': row(g('rwkv_a0')),
        'rwkv_kk': row(g('rwkv_kk')), 'rwkv_ka': row(g('rwkv_ka')),
        'rwkv_w2p': lora_pad(g('rwkv_w2'), 0),
        'rwkv_a2p': lora_pad(g('rwkv_a2'), RWKV_DECAY_LORA),
        'rwkv_g2p': lora_pad(g('rwkv_g2'), RWKV_DECAY_LORA + RWKV_AAA_LORA),
        'rwkv_rk': row(g('rwkv_rk')), 'rwkv_ln_g': row(g('rwkv_ln_g')), 'rwkv_ln_b': row(g('rwkv_ln_b')),
        'pool_wbd': pool_wbd.astype(BF16), 'pool_scale': row(g('pool_scale')),
        'conv_w': g('conv_w'), 'conv_b': row(g('conv_b')),
        'conv_ln_g': row(g('conv_ln_g')), 'conv_ln_b': row(g('conv_ln_b')),
        'ssm_conv_w': g('ssm_conv_w'), 'ssm_conv_b': row(g('ssm_conv_b')),
        'ssm_dt_bias_pad': jnp.pad(row(g('ssm_dt_bias')), ((0, 0), (0, LANES - SSM_HEADS))),
        'ssm_a_log_e': row(jnp.repeat(g('ssm_a_log'), SSM_HEAD_DIM)),
        'ssm_d_e': row(jnp.repeat(g('ssm_d'), SSM_HEAD_DIM)),
        'ssm_norm_g': row(g('ssm_norm_g')),
        'w_branch': g('w_branch').astype(BF16), 'w_out': g('w_out').astype(BF16),
        'norm_ffn_g': row(g('norm_ffn_g')),
        'router_w': jnp.pad(router_w, ((0, 0), (0, ROUTER_PAD - n_r))),
        'router_b': jnp.pad(row(router_b), ((0, 0), (0, ROUTER_PAD - n_r))),
        'moe_w_gu': jnp.concatenate([g('moe_w_gate'), g('moe_w_up')], axis=-1).astype(BF16),
        'moe_w_down': g('moe_w_down').astype(BF16),
    }


def _layer(x2, st, b, l, pos0, lw, final_g, final_norm):
    shift0, wkv0, pool0, conv0, sconv0, ssm0 = st
    h, p_rwkv, u_pool, s_conv, zx = _inproj(x2, lw['norm_mix_g'], lw['w_mix'])
    seq = lambda a: a.reshape(b, l, a.shape[-1])

    p3 = seq(p_rwkv)
    o_rwkv, wkv1t = _rwkv_scan(_rwkv_prep(p3, shift0, lw), lw, _wkv_to_pairs(wkv0))
    o_pool, pool1 = _pool(seq(u_pool), pool0, pos0, lw)
    o_conv, conv1 = _conv(seq(s_conv), conv0, lw)
    zx3 = seq(zx)
    xbc_raw = zx3[:, :, MIX:MIX + SSM_CONV_DIM]
    if l >= SSM_CONV_HIST:
        sconv1 = xbc_raw[:, l - SSM_CONV_HIST:]
    else:
        sconv1 = jnp.concatenate([sconv0, xbc_raw], axis=1)[:, -SSM_CONV_HIST:]
    lpad = -l % SUBLANES
    zx3p = jnp.pad(zx3, ((0, 0), (0, lpad), (0, 0))) if lpad else zx3
    o_ssm, ssm1t = _ssd(zx3p, sconv0, _ssm_to_t(ssm0), l, lw)
    o_ssm = o_ssm[:, :l]

    flat = lambda a: a.reshape(b * l, MIX)
    x1, h2, logits = _merge(x2, h, [flat(o_rwkv), flat(o_pool), flat(o_conv), flat(o_ssm)], lw)
    x_out = _moe(h2, logits, x1, lw, final_g, final_norm)
    return x_out, (p3[:, -1], _wkv_from_pairs(wkv1t), pool1, conv1, sconv1, _ssm_from_t(ssm1t))


def _trunk(x, states, pos0, layers, final_g):
    b, l, _ = x.shape
    x2 = x.reshape(b * l, D_MODEL)
    new = [[] for _ in states]
    for li, lw in enumerate(layers):
        st = tuple(s[li] for s in states)
        x2, st1 = _layer(x2, st, b, l, pos0, lw, final_g, li == len(layers) - 1)
        for lst, s in zip(new, st1):
            lst.append(s)
    return x2.reshape(b, l, D_MODEL), tuple(jnp.stack(lst) for lst in new)


def kernel(x_prompt, x_sample, state_rwkv_shift, state_rwkv_wkv, state_pool, state_conv, state_ssm_conv, state_ssm,
           norm_mix_g, w_in, rwkv_mu, rwkv_w0, rwkv_w2, rwkv_a0, rwkv_a2, rwkv_g2, rwkv_kk, rwkv_ka, rwkv_rk,
           rwkv_ln_g, rwkv_ln_b, pool_w, pool_scale, conv_w, conv_b, conv_ln_g, conv_ln_b, ssm_conv_w, ssm_conv_b,
           ssm_dt_bias, ssm_a_log, ssm_d, ssm_norm_g, w_branch, w_out, norm_ffn_g, moe_router_group_w,
           moe_router_group_b, moe_router_expert_w, moe_router_expert_b, moe_w_gate, moe_w_up, moe_w_down,
           final_norm_g):
    params = {
        'norm_mix_g': norm_mix_g, 'w_in': w_in, 'rwkv_mu': rwkv_mu, 'rwkv_w0': rwkv_w0, 'rwkv_w2': rwkv_w2,
        'rwkv_a0': rwkv_a0, 'rwkv_a2': rwkv_a2, 'rwkv_g2': rwkv_g2, 'rwkv_kk': rwkv_kk, 'rwkv_ka': rwkv_ka,
        'rwkv_rk': rwkv_rk, 'rwkv_ln_g': rwkv_ln_g, 'rwkv_ln_b': rwkv_ln_b, 'pool_w': pool_w,
        'pool_scale': pool_scale, 'conv_w': conv_w, 'conv_b': conv_b, 'conv_ln_g': conv_ln_g,
        'conv_ln_b': conv_ln_b, 'ssm_conv_w': ssm_conv_w, 'ssm_conv_b': ssm_conv_b, 'ssm_dt_bias': ssm_dt_bias,
        'ssm_a_log': ssm_a_log, 'ssm_d': ssm_d, 'ssm_norm_g': ssm_norm_g, 'w_branch': w_branch, 'w_out': w_out,
        'norm_ffn_g': norm_ffn_g, 'moe_router_group_w': moe_router_group_w,
        'moe_router_group_b': moe_router_group_b, 'moe_router_expert_w': moe_router_expert_w,
        'moe_router_expert_b': moe_router_expert_b, 'moe_w_gate': moe_w_gate, 'moe_w_up': moe_w_up,
        'moe_w_down': moe_w_down,
    }
    layers = [_prep_layer(params, l) for l in range(DEPTH)]
    final_g = final_norm_g.reshape(1, D_MODEL).astype(F32)
    bp, dtp = x_prompt.shape[0], x_prompt.dtype
    empty = (jnp.zeros((DEPTH, bp, RWKV_PROJ), dtp),
             jnp.zeros((DEPTH, bp, RWKV_HEADS, RWKV_HEAD, RWKV_HEAD), dtp),
             jnp.zeros((DEPTH, bp, POOL_HIST, MIX), dtp),
             jnp.zeros((DEPTH, bp, CONV_HIST, MIX), dtp),
             jnp.zeros((DEPTH, bp, SSM_CONV_HIST, SSM_CONV_DIM), dtp),
             jnp.zeros((DEPTH, bp, SSM_HEADS, SSM_HEAD_DIM, SSM_STATE), dtp))
    y_prompt, p_states = _trunk(x_prompt, empty, 0, layers, final_g)
    past = (state_rwkv_shift, state_rwkv_wkv, state_pool, state_conv, state_ssm_conv, state_ssm)
    y_sample, s_states = _trunk(x_sample, past, PAST_LEN, layers, final_g)
    return (y_prompt, y_sample) + tuple(p_states) + tuple(s_states)
```

```python
import functools
import math

import jax
import jax.numpy as jnp
from jax import lax
from jax.experimental import pallas as pl
from jax.experimental.pallas import tpu as pltpu

F32 = jnp.float32
BF16 = jnp.bfloat16
HIGHEST = lax.Precision.HIGHEST

D_MODEL = 1024
DEPTH = 2
PAST_LEN = 16384
MIX = D_MODEL // 4
N_BRANCH = 4
RWKV_HEAD = 64
RWKV_HEADS = MIX // RWKV_HEAD
RWKV_DECAY_LORA = 32
RWKV_AAA_LORA = 32
RWKV_GATE_LORA = 64
RWKV_PROJ = 3 * MIX + RWKV_DECAY_LORA + RWKV_AAA_LORA + RWKV_GATE_LORA
RWKV_GN_EPS = 64e-5
POOL_WINDOWS = (2, 4, 8, 16)
POOL_HIST = 15
CONV_KERNEL = 31
CONV_HIST = CONV_KERNEL - 1
CONV_LN_EPS = 1e-5
SSM_HEAD_DIM = 64
SSM_HEADS = MIX // SSM_HEAD_DIM
SSM_GROUPS = 2
SSM_STATE = 128
SSM_CONV_KERNEL = 4
SSM_CONV_HIST = SSM_CONV_KERNEL - 1
SSM_CONV_DIM = MIX + 2 * SSM_GROUPS * SSM_STATE
SSM_PROJ = MIX + SSM_CONV_DIM + SSM_HEADS
OFF_POOL = RWKV_PROJ
OFF_CONV = OFF_POOL + MIX
OFF_SSM = OFF_CONV + 2 * MIX
OFF_GATE = OFF_SSM + SSM_PROJ
MOE_GROUPS = 4
MOE_PER_GROUP = 4
MOE_EXPERTS = MOE_GROUPS * MOE_PER_GROUP
MOE_HIDDEN = 256
NORM_EPS = 1e-6

LANES = 128
SUBLANES = 8
VMEM_LIMIT_BYTES = 48 * 1024 * 1024

SSM_IN_PAD = 9 * LANES
MIX_IN_PAD = OFF_SSM + SSM_IN_PAD
ROUTER_PAD = LANES
SSD_CHUNK = 128
RWKV_CHUNK = 16
RWKV_BATCH_BLOCK = 8
NEG_BIG = -1e30


def _sigmoid(x):
    return 1.0 / (1.0 + jnp.exp(-x))


def _silu(x):
    return x * _sigmoid(x)


def _softplus(x):
    return jnp.maximum(x, 0.0) + jnp.log(1.0 + jnp.exp(-jnp.abs(x)))


def _dot(a, b):
    return jnp.dot(a, b, preferred_element_type=F32)


def _dot_hi(a, b):
    return jnp.dot(a, b, preferred_element_type=F32, precision=HIGHEST)


def _dot_nt(a, b, precision=None):
    return lax.dot_general(a, b, (((1,), (1,)), ((), ())), preferred_element_type=F32, precision=precision)


def _eye(n):
    return (lax.broadcasted_iota(jnp.int32, (n, n), 0) == lax.broadcasted_iota(jnp.int32, (n, n), 1)).astype(F32)


def _head_ones(n, head):
    r = lax.broadcasted_iota(jnp.int32, (n, n), 0) // head
    c = lax.broadcasted_iota(jnp.int32, (n, n), 1) // head
    return (r == c).astype(F32)


def _params(sem):
    return pltpu.CompilerParams(dimension_semantics=sem, vmem_limit_bytes=VMEM_LIMIT_BYTES)


def _const_spec(shape):
    nd = len(shape)
    return pl.BlockSpec(shape, lambda *_: (0,) * nd)


def _inproj_kernel(x_ref, g_ref, w_ref, h_ref, rw_ref, pool_ref, conv_ref, ssm_ref):
    x = x_ref[...]
    y = x * lax.rsqrt(jnp.mean(x * x, -1, keepdims=True) + NORM_EPS) * g_ref[...]
    hb = y.astype(BF16)
    h_ref[...] = hb
    rw_ref[...] = _dot(hb, w_ref[:, 0:OFF_POOL])
    pool_ref[...] = _dot(hb, w_ref[:, OFF_POOL:OFF_CONV])
    conv_ref[...] = _dot(hb, w_ref[:, OFF_CONV:OFF_SSM])
    ssm_ref[...] = _dot(hb, w_ref[:, OFF_SSM:MIX_IN_PAD])


def _inproj(x2, g, w_mix):
    t = x2.shape[0]
    tm = min(t, 512)
    row = lambda w: pl.BlockSpec((tm, w), lambda i: (i, 0))
    return pl.pallas_call(
        _inproj_kernel,
        grid=(t // tm,),
        in_specs=[row(D_MODEL), _const_spec((1, D_MODEL)), _const_spec((D_MODEL, MIX_IN_PAD))],
        out_specs=[row(D_MODEL), row(RWKV_PROJ), row(MIX), row(2 * MIX), row(SSM_IN_PAD)],
        out_shape=[jax.ShapeDtypeStruct((t, D_MODEL), BF16),
                   jax.ShapeDtypeStruct((t, RWKV_PROJ), F32),
                   jax.ShapeDtypeStruct((t, MIX), F32),
                   jax.ShapeDtypeStruct((t, 2 * MIX), F32),
                   jax.ShapeDtypeStruct((t, SSM_IN_PAD), F32)],
        compiler_params=_params(("parallel",)),
        name="inproj",
    )(x2, g, w_mix)


def _rwkv_prep_math(p, prev, mu, w0, a0, kkw, kaw, w2p, a2p, g2p):
    q = p + (prev - p) * mu
    r = q[:, 0:MIX]
    k = q[:, MIX:2 * MIX]
    v = q[:, 2 * MIX:3 * MIX]
    lora = q[:, 3 * MIX:RWKV_PROJ]
    zw = w0 + _dot_hi(jnp.tanh(lora), w2p)
    decay = jnp.exp(-math.exp(-0.5) * _sigmoid(zw))
    a = _sigmoid(a0 + _dot_hi(lora, a2p))
    g = _dot_hi(_sigmoid(lora), g2p)
    kk = k * kkw
    ss = _dot_hi(kk * kk, _head_ones(MIX, RWKV_HEAD))
    kk = kk * lax.rsqrt(jnp.maximum(ss, 1e-24))
    k2 = k * (1.0 + (a - 1.0) * kaw)
    return r, k2, v, kk, kk * a, decay, g


def _rwkv_prep_seq_kernel(p_ref, s0_ref, mu_ref, w0_ref, a0_ref, kkw_ref, kaw_ref, w2_ref, a2_ref, g2_ref,
                          r_ref, k_ref, v_ref, kk_ref, b_ref, d_ref, g_ref, carry_ref):
    @pl.when(pl.program_id(1) == 0)
    def _():
        carry_ref[...] = s0_ref[...]

    p = p_ref[...]
    rows = p.shape[0]
    first = lax.broadcasted_iota(jnp.int32, p.shape, 0) == 0
    prev = jnp.where(first, carry_ref[...], pltpu.roll(p, 1, 0))
    carry_ref[...] = p[rows - 1:rows, :]
    outs = _rwkv_prep_math(p, prev, mu_ref[...], w0_ref[...], a0_ref[...], kkw_ref[...], kaw_ref[...],
                           w2_ref[...], a2_ref[...], g2_ref[...])
    for ref, val in zip((r_ref, k_ref, v_ref, kk_ref, b_ref, d_ref, g_ref), outs):
        ref[...] = val


def _rwkv_prep_tok_kernel(p_ref, prev_ref, mu_ref, w0_ref, a0_ref, kkw_ref, kaw_ref, w2_ref, a2_ref, g2_ref,
                          r_ref, k_ref, v_ref, kk_ref, b_ref, d_ref, g_ref):
    outs = _rwkv_prep_math(p_ref[...], prev_ref[...], mu_ref[...], w0_ref[...], a0_ref[...], kkw_ref[...],
                           kaw_ref[...], w2_ref[...], a2_ref[...], g2_ref[...])
    for ref, val in zip((r_ref, k_ref, v_ref, kk_ref, b_ref, d_ref, g_ref), outs):
        ref[...] = val


def _rwkv_prep(p3, shift0, lw):
    b, l, _ = p3.shape
    consts = [lw['rwkv_mu'], lw['rwkv_w0'], lw['rwkv_a0'], lw['rwkv_kk'], lw['rwkv_ka'],
              lw['rwkv_w2p'], lw['rwkv_a2p'], lw['rwkv_g2p']]
    const_specs = [_const_spec(c.shape) for c in consts]
    if l == 1:
        outs = pl.pallas_call(
            _rwkv_prep_tok_kernel,
            grid=(1,),
            in_specs=[_const_spec((b, RWKV_PROJ)), _const_spec((b, RWKV_PROJ))] + const_specs,
            out_specs=[_const_spec((b, MIX))] * 7,
            out_shape=[jax.ShapeDtypeStruct((b, MIX), F32)] * 7,
            compiler_params=_params(("arbitrary",)),
            name="rwkv_prep_tok",
        )(p3[:, 0], shift0, *consts)
        return [o[:, None, :] for o in outs]
    lt = min(l, 512)
    seq = lambda w: pl.BlockSpec((None, lt, w), lambda i, j: (i, j, 0))
    return pl.pallas_call(
        _rwkv_prep_seq_kernel,
        grid=(b, l // lt),
        in_specs=[seq(RWKV_PROJ), pl.BlockSpec((None, 1, RWKV_PROJ), lambda i, j: (i, 0, 0))] + const_specs,
        out_specs=[seq(MIX)] * 7,
        out_shape=[jax.ShapeDtypeStruct((b, l, MIX), F32)] * 7,
        scratch_shapes=[pltpu.VMEM((1, RWKV_PROJ), F32)],
        compiler_params=_params(("parallel", "arbitrary")),
        name="rwkv_prep_seq",
    )(p3, shift0[:, None, :], *consts)


def _rwkv_scan_kernel(r_ref, k_ref, v_ref, kk_ref, b_ref, d_ref, g_ref, lng_ref, lnb_ref, rk_ref, s0_ref,
                      o_ref, s1_ref, st_ref, col_ref, row_ref, y_ref, *, nb, lc):
    c = pl.program_id(2)
    n = RWKV_HEAD

    @pl.when(c == 0)
    def _():
        for bb in range(nb):
            st_ref[bb] = _dot_nt(_eye(n), s0_ref[bb], HIGHEST)

    ones = _head_ones(LANES, n).astype(BF16)
    ones3 = jnp.concatenate([ones, ones, ones], axis=0)
    lane = lax.broadcasted_iota(jnp.int32, (n, LANES), 1)
    sub = lax.broadcasted_iota(jnp.int32, (n, LANES), 0)
    diag = (lane % n == sub).astype(BF16)
    low = lax.broadcasted_iota(jnp.int32, (lc, LANES), 1) < n

    def head_sum(x):
        s_lo = jnp.sum(jnp.where(low, x, 0.0), -1, keepdims=True)
        s_hi = jnp.sum(jnp.where(low, 0.0, x), -1, keepdims=True)
        return jnp.where(low, s_lo, s_hi)

    def spread3(x):
        hi = x.astype(BF16)
        r1 = x - hi.astype(F32)
        mid = r1.astype(BF16)
        lo = (r1 - mid.astype(F32)).astype(BF16)
        parts = [(p[:, None, :] * diag[None, :, :]).reshape(lc * n, LANES) for p in (hi, mid, lo)]
        return jnp.concatenate(parts, axis=-1)

    for bb in range(nb):
        r = r_ref[bb]
        k = k_ref[bb]
        kk = kk_ref[bb]
        bv = b_ref[bb]
        d = d_ref[bb]
        for idx, x in enumerate((kk, d, bv, k, d * r)):
            col_ref[idx, bb] = _dot(spread3(x), ones3)
        row_ref[0, bb] = head_sum(bv * r)
        row_ref[1, bb] = head_sum(k * r)

    def step(t, carry):
        base = pl.multiple_of(t * n, n)
        for bb in range(nb):
            st = st_ref[bb]
            ckk = col_ref[0, bb, pl.ds(base, n), :]
            cd = col_ref[1, bb, pl.ds(base, n), :]
            cb = col_ref[2, bb, pl.ds(base, n), :]
            ck = col_ref[3, bb, pl.ds(base, n), :]
            cdr = col_ref[4, bb, pl.ds(base, n), :]
            vrow = v_ref[bb, pl.ds(t, 1), :]
            br = row_ref[0, bb, pl.ds(t, 1), :]
            kr = row_ref[1, bb, pl.ds(t, 1), :]
            sa = -jnp.sum(st * ckk, axis=0, keepdims=True)
            y = jnp.sum(st * cdr, axis=0, keepdims=True) + sa * br + vrow * kr
            st_ref[bb] = st * cd + sa * cb + vrow * ck
            y_ref[bb, pl.ds(t, 1), :] = y
        return carry

    lax.fori_loop(0, lc, step, 0)

    lng = lng_ref[...]
    lnb = lnb_ref[...]
    rk = rk_ref[...]
    inv_n = 1.0 / n
    for bb in range(nb):
        y = y_ref[bb]
        mu = head_sum(y) * inv_n
        yc = y - mu
        var = head_sum(yc * yc) * inv_n
        yn = yc * lax.rsqrt(var + RWKV_GN_EPS) * lng + lnb
        r = r_ref[bb]
        v = v_ref[bb]
        rkv = head_sum(r * k_ref[bb] * rk)
        o_ref[bb] = (yn + rkv * v) * g_ref[bb]

    @pl.when(c == pl.num_programs(2) - 1)
    def _():
        for bb in range(nb):
            s1_ref[bb] = _dot_nt(_eye(LANES), st_ref[bb], HIGHEST)


def _rwkv_scan(prep, lw, s0t):
    r, k, v, kk, bv, d, g = prep
    b, l, _ = r.shape
    nb = RWKV_BATCH_BLOCK
    lc = min(l, RWKV_CHUNK)
    n = RWKV_HEAD
    seq = pl.BlockSpec((nb, lc, LANES), lambda i, p, c: (i, c, p))
    vec = pl.BlockSpec((1, LANES), lambda i, p, c: (0, p))
    st = pl.BlockSpec((nb, None, LANES, n), lambda i, p, c: (i, p, 0, 0))
    kern = functools.partial(_rwkv_scan_kernel, nb=nb, lc=lc)
    return pl.pallas_call(
        kern,
        grid=(b // nb, 2, l // lc),
        in_specs=[seq] * 7 + [vec] * 3 + [st],
        out_specs=[seq, st],
        out_shape=[jax.ShapeDtypeStruct((b, l, MIX), F32), jax.ShapeDtypeStruct((b, 2, LANES, n), F32)],
        scratch_shapes=[pltpu.VMEM((nb, n, LANES), F32),
                        pltpu.VMEM((5, nb, lc * n, LANES), F32),
                        pltpu.VMEM((2, nb, lc, LANES), F32),
                        pltpu.VMEM((nb, lc, LANES), F32)],
        compiler_params=_params(("parallel", "parallel", "arbitrary")),
        name="rwkv_scan",
    )(r, k, v, kk, bv, d, g, lw['rwkv_ln_g'], lw['rwkv_ln_b'], lw['rwkv_rk'], s0t)


def _wkv_to_pairs(s):
    return s.reshape(s.shape[0], 2, 2 * RWKV_HEAD, RWKV_HEAD)


def _wkv_from_pairs(s):
    return s.reshape(s.shape[0], RWKV_HEADS, RWKV_HEAD, RWKV_HEAD)


def _pool_math(tap, pos, pw, scale):
    rows = pos.shape[0]
    lane = lax.broadcasted_iota(jnp.int32, (rows, LANES), 1)
    low = lane < (LANES // 2)
    cnt = lambda w: jnp.minimum(pos + 1, w).astype(F32)
    a0 = tap(0, 0)
    s2 = a0 + tap(1, 0)
    s4 = s2 + tap(2, 0) + tap(3, 0)
    b0 = tap(0, 1)
    s8 = b0
    for k in range(1, 8):
        s8 = s8 + tap(k, 1)
    s16 = s8
    for k in range(8, 16):
        s16 = s16 + tap(k, 1)
    diff_a = jnp.where(low, s2 / cnt(2), s4 / cnt(4)) - a0
    diff_b = jnp.where(low, s8 / cnt(8), s16 / cnt(16)) - b0
    diff = jnp.concatenate([diff_a, diff_b], axis=-1).astype(BF16)
    return _dot(diff, pw) * scale


def _pool_seq_kernel(u_ref, h_ref, pw_ref, sc_ref, o_ref, h1_ref, full_ref, *, l, tt):
    pad = POOL_HIST + 1
    full_ref[0:1, :] = jnp.zeros((1, MIX), F32)
    full_ref[1:pad, :] = h_ref[...]
    full_ref[pad:pad + l, :] = u_ref[...]
    pw = pw_ref[...]
    sc = sc_ref[...]
    for i in range(l // tt):
        t0 = i * tt
        tap = lambda k, half: full_ref[pad + t0 - k:pad + t0 - k + tt, half * LANES:(half + 1) * LANES]
        pos = t0 + lax.broadcasted_iota(jnp.int32, (tt, 1), 0)
        o_ref[t0:t0 + tt, :] = _pool_math(tap, pos, pw, sc)
    h1_ref[...] = full_ref[l + 1:l + pad, :]


def _pool_tok_kernel(u_ref, ht_ref, pw_ref, sc_ref, o_ref, *, pos0):
    def tap(k, half):
        sl = slice(half * LANES, (half + 1) * LANES)
        return u_ref[:, sl] if k == 0 else ht_ref[POOL_HIST - k, :, sl]
    rows = u_ref.shape[0]
    pos = jnp.full((rows, 1), pos0, jnp.int32)
    o_ref[...] = _pool_math(tap, pos, pw_ref[...], sc_ref[...])


def _pool(u3, hist, pos0, lw):
    b, l, _ = u3.shape
    consts = [lw['pool_wbd'], lw['pool_scale']]
    const_specs = [_const_spec(c.shape) for c in consts]
    if l == 1:
        ht = jnp.swapaxes(hist, 0, 1)
        out = pl.pallas_call(
            functools.partial(_pool_tok_kernel, pos0=pos0),
            grid=(1,),
            in_specs=[_const_spec((b, MIX)), _const_spec((POOL_HIST, b, MIX))] + const_specs,
            out_specs=_const_spec((b, MIX)),
            out_shape=jax.ShapeDtypeStruct((b, MIX), F32),
            compiler_params=_params(("arbitrary",)),
            name="pool_tok",
        )(u3[:, 0], ht, *consts)
        return out[:, None, :], jnp.concatenate([hist[:, 1:], u3], axis=1)
    assert pos0 == 0 and l >= POOL_HIST
    tt = min(l, 256)
    seq = pl.BlockSpec((None, l, MIX), lambda i: (i, 0, 0))
    hs = pl.BlockSpec((None, POOL_HIST, MIX), lambda i: (i, 0, 0))
    return pl.pallas_call(
        functools.partial(_pool_seq_kernel, l=l, tt=tt),
        grid=(b,),
        in_specs=[seq, hs] + const_specs,
        out_specs=[seq, hs],
        out_shape=[jax.ShapeDtypeStruct((b, l, MIX), F32), jax.ShapeDtypeStruct((b, POOL_HIST, MIX), F32)],
        scratch_shapes=[pltpu.VMEM((l + POOL_HIST + 1, MIX), F32)],
        compiler_params=_params(("parallel",)),
        name="pool_seq",
    )(u3, hist, *consts)


def _conv_post(y, cb, g, beta):
    y = y + cb
    mu = jnp.mean(y, -1, keepdims=True)
    yc = y - mu
    var = jnp.mean(yc * yc, -1, keepdims=True)
    return _silu(yc * lax.rsqrt(var + CONV_LN_EPS) * g + beta)


def _glu(s):
    return s[:, 0:MIX] * _sigmoid(s[:, MIX:2 * MIX])


def _conv_seq_kernel(s_ref, h_ref, w_ref, cb_ref, g_ref, beta_ref, o_ref, h1_ref, full_ref, *, l, tt):
    pad = CONV_HIST + 2
    full_ref[0:2, :] = jnp.zeros((2, MIX), F32)
    full_ref[2:pad, :] = h_ref[...]
    full_ref[pad:pad + l, :] = _glu(s_ref[...])
    cb = cb_ref[...]
    g = g_ref[...]
    beta = beta_ref[...]

    for i in range(l // tt):
        t0 = i * tt
        acc = jnp.zeros((tt, MIX), F32)
        for k in range(CONV_KERNEL):
            acc = acc + w_ref[k:k + 1, :] * full_ref[t0 + k + 2:t0 + k + 2 + tt, :]
        o_ref[t0:t0 + tt, :] = _conv_post(acc, cb, g, beta)
    h1_ref[...] = full_ref[l + 2:l + pad, :]


def _conv_tok_kernel(s_ref, ht_ref, w_ref, cb_ref, g_ref, beta_ref, o_ref, c_ref):
    c = _glu(s_ref[...])
    c_ref[...] = c
    acc = w_ref[CONV_HIST:CONV_KERNEL, :] * c
    for k in range(CONV_HIST):
        acc = acc + w_ref[k:k + 1, :] * ht_ref[k]
    o_ref[...] = _conv_post(acc, cb_ref[...], g_ref[...], beta_ref[...])


def _conv(s3, hist, lw):
    b, l, _ = s3.shape
    consts = [lw['conv_w'], lw['conv_b'], lw['conv_ln_g'], lw['conv_ln_b']]
    const_specs = [_const_spec(c.shape) for c in consts]
    if l == 1:
        ht = jnp.swapaxes(hist, 0, 1)
        out, c = pl.pallas_call(
            _conv_tok_kernel,
            grid=(1,),
            in_specs=[_const_spec((b, 2 * MIX)), _const_spec((CONV_HIST, b, MIX))] + const_specs,
            out_specs=[_const_spec((b, MIX))] * 2,
            out_shape=[jax.ShapeDtypeStruct((b, MIX), F32)] * 2,
            compiler_params=_params(("arbitrary",)),
            name="conv_tok",
        )(s3[:, 0], ht, *consts)
        return out[:, None, :], jnp.concatenate([hist[:, 1:], c[:, None, :]], axis=1)
    assert l >= CONV_HIST
    tt = min(l, 64)
    hs = pl.BlockSpec((None, CONV_HIST, MIX), lambda i: (i, 0, 0))
    return pl.pallas_call(
        functools.partial(_conv_seq_kernel, l=l, tt=tt),
        grid=(b,),
        in_specs=[pl.BlockSpec((None, l, 2 * MIX), lambda i: (i, 0, 0)), hs] + const_specs,
        out_specs=[pl.BlockSpec((None, l, MIX), lambda i: (i, 0, 0)), hs],
        out_shape=[jax.ShapeDtypeStruct((b, l, MIX), F32), jax.ShapeDtypeStruct((b, CONV_HIST, MIX), F32)],
        scratch_shapes=[pltpu.VMEM((l + CONV_HIST + 2, MIX), F32)],
        compiler_params=_params(("parallel",)),
        name="conv_seq",
    )(s3, hist, *consts)


def _ssd_kernel(zx_ref, ch_ref, h0_ref, cw_ref, cb_ref, dtb_ref, alog_ref, dd_ref, ng_ref,
                o_ref, h1_ref, xf_ref, ht_ref, *, lb, lvalid):
    q = SSD_CHUNK
    c = pl.program_id(1)
    halo = SUBLANES
    gw = SSM_GROUPS * SSM_STATE

    @pl.when(c == 0)
    def _():
        ht_ref[...] = h0_ref[...]
        xf_ref[...] = jnp.zeros(xf_ref.shape, F32)
        xf_ref[halo - SSM_CONV_HIST:halo, :] = ch_ref[...]

    xf_ref[halo:halo + lb, :] = zx_ref[:, MIX:MIX + SSM_CONV_DIM]
    acc = jnp.zeros((q, SSM_CONV_DIM), F32)
    for k in range(SSM_CONV_KERNEL):
        acc = acc + cw_ref[k:k + 1, :] * xf_ref[halo - SSM_CONV_HIST + k:halo - SSM_CONV_HIST + k + q, :]
    xbc = _silu(acc + cb_ref[...])
    xf_ref[halo - SSM_CONV_HIST:halo, :] = xf_ref[halo + q - SSM_CONV_HIST:halo + q, :]

    xs = xbc[:, 0:MIX]
    bm = xbc[:, MIX:MIX + gw]
    cm = xbc[:, MIX + gw:MIX + 2 * gw]

    hsel = (lax.broadcasted_iota(jnp.int32, (LANES, MIX), 0)
            == lax.broadcasted_iota(jnp.int32, (LANES, MIX), 1) // SSM_HEAD_DIM).astype(F32)
    dt_raw = zx_ref[:, MIX + SSM_CONV_DIM:SSM_IN_PAD]
    if lb < q:
        dt_raw = jnp.concatenate([dt_raw, jnp.zeros((q - lb, LANES), F32)], axis=0)
    trow = lax.broadcasted_iota(jnp.int32, (q, 1), 0) + c * q
    dt = jnp.where(trow < lvalid, _softplus(dt_raw + dtb_ref[...]), 0.0)
    dte = _dot_hi(dt, hsel)
    dae = dte * (-jnp.exp(alog_ref[...]))
    ti = lax.broadcasted_iota(jnp.int32, (q, q), 0)
    tj = lax.broadcasted_iota(jnp.int32, (q, q), 1)
    causal = ti >= tj
    acse = _dot_hi(causal.astype(F32), dae)
    xdt = xs * dte
    last = acse[q - 1:q, :]
    w_end = xdt * jnp.exp(last - acse)
    eacs = jnp.exp(acse)

    lane_g = lax.broadcasted_iota(jnp.int32, (q, LANES), 1)
    heads_per_group = SSM_HEADS // SSM_GROUPS
    ys = []
    for g in range(SSM_GROUPS):
        lo = g * LANES
        bg = bm[:, g * SSM_STATE:(g + 1) * SSM_STATE].astype(BF16)
        cg = cm[:, g * SSM_STATE:(g + 1) * SSM_STATE].astype(BF16)
        gmat = _dot_nt(cg, bg)
        hg = ht_ref[lo:lo + LANES, :]
        yg = _dot_nt(cg, hg.astype(BF16)) * eacs[:, lo:lo + LANES]
        xg = xdt[:, lo:lo + LANES]
        for hh in range(heads_per_group):
            col = acse[:, lo + hh * SSM_HEAD_DIM:lo + hh * SSM_HEAD_DIM + 1]
            row = jnp.transpose(jnp.broadcast_to(col, (q, q)))
            seg = jnp.where(causal, col - row, NEG_BIG)
            scores = (gmat * jnp.exp(seg)).astype(BF16)
            xm = jnp.where(lane_g // SSM_HEAD_DIM == hh, xg, 0.0).astype(BF16)
            yg = yg + _dot(scores, xm)
        ys.append(yg)
        st = _dot(jnp.transpose(w_end[:, lo:lo + LANES]).astype(BF16), bg)
        for hh in range(heads_per_group):
            r0 = hh * SSM_HEAD_DIM
            decay = jnp.exp(last[:, lo + r0:lo + r0 + 1])
            ht_ref[lo + r0:lo + r0 + SSM_HEAD_DIM, :] = hg[r0:r0 + SSM_HEAD_DIM] * decay + st[r0:r0 + SSM_HEAD_DIM]
    y = jnp.concatenate(ys, axis=-1)

    y = y + dd_ref[...] * xs
    z = zx_ref[:, 0:MIX]
    if lb < q:
        z = jnp.concatenate([z, jnp.zeros((q - lb, MIX), F32)], axis=0)
    y = y * _silu(z)
    y = y * lax.rsqrt(jnp.mean(y * y, -1, keepdims=True) + NORM_EPS) * ng_ref[...]
    o_ref[...] = y[0:lb, :]

    @pl.when(c == pl.num_programs(1) - 1)
    def _():
        h1_ref[...] = ht_ref[...]


def _ssd(zx3, conv_hist, h0t, lvalid, lw):
    b, lp, _ = zx3.shape
    q = SSD_CHUNK
    lb = min(lp, q)
    consts = [lw['ssm_conv_w'], lw['ssm_conv_b'], lw['ssm_dt_bias_pad'], lw['ssm_a_log_e'], lw['ssm_d_e'],
              lw['ssm_norm_g']]
    const_specs = [pl.BlockSpec(cst.shape, lambda i, c: (0, 0)) for cst in consts]
    hs = pl.BlockSpec((None, MIX, SSM_STATE), lambda i, c: (i, 0, 0))
    return pl.pallas_call(
        functools.partial(_ssd_kernel, lb=lb, lvalid=lvalid),
        grid=(b, lp // lb),
        in_specs=[pl.BlockSpec((None, lb, SSM_IN_PAD), lambda i, c: (i, c, 0)),
                  pl.BlockSpec((None, SSM_CONV_HIST, SSM_CONV_DIM), lambda i, c: (i, 0, 0)), hs] + const_specs,
        out_specs=[pl.BlockSpec((None, lb, MIX), lambda i, c: (i, c, 0)), hs],
        out_shape=[jax.ShapeDtypeStruct((b, lp, MIX), F32), jax.ShapeDtypeStruct((b, MIX, SSM_STATE), F32)],
        scratch_shapes=[pltpu.VMEM((SUBLANES + q, SSM_CONV_DIM), F32), pltpu.VMEM((MIX, SSM_STATE), F32)],
        compiler_params=_params(("parallel", "arbitrary")),
        name="ssd",
    )(zx3, conv_hist, h0t, *consts)


def _ssm_to_rows(s):
    return s.reshape(s.shape[0], MIX, SSM_STATE)


def _ssm_from_rows(s):
    return s.reshape(s.shape[0], SSM_HEADS, SSM_HEAD_DIM, SSM_STATE)


def _merge_kernel(x_ref, h_ref, b0_ref, b1_ref, b2_ref, b3_ref, wg_ref, wb_ref, wo_ref, ng_ref, wr_ref, br_ref,
                  x1_ref, h2_ref, lg_ref):
    h = h_ref[...]
    merged = None
    for kbr, b_ref in enumerate((b0_ref, b1_ref, b2_ref, b3_ref)):
        gate = _sigmoid(_dot(h, wg_ref[:, kbr * D_MODEL:(kbr + 1) * D_MODEL]))
        up = _dot(b_ref[...].astype(BF16), wb_ref[kbr])
        merged = up * gate if merged is None else merged + up * gate
    x1 = x_ref[...] + _dot(merged.astype(BF16), wo_ref[...])
    x1_ref[...] = x1
    hn = x1 * lax.rsqrt(jnp.mean(x1 * x1, -1, keepdims=True) + NORM_EPS) * ng_ref[...]
    h2_ref[...] = hn.astype(BF16)
    lg_ref[...] = _dot_hi(hn, wr_ref[...]) + br_ref[...]


def _merge(x2, h2d, branches, lw):
    t = x2.shape[0]
    tm = min(t, 256)
    row = lambda w: pl.BlockSpec((tm, w), lambda i: (i, 0))
    consts = [lw['w_gate'], lw['w_branch'], lw['w_out'], lw['norm_ffn_g'], lw['router_w'], lw['router_b']]
    return pl.pallas_call(
        _merge_kernel,
        grid=(t // tm,),
        in_specs=[row(D_MODEL), row(D_MODEL)] + [row(MIX)] * 4 + [_const_spec(c.shape) for c in consts],
        out_specs=[row(D_MODEL), row(D_MODEL), row(ROUTER_PAD)],
        out_shape=[jax.ShapeDtypeStruct((t, D_MODEL), F32), jax.ShapeDtypeStruct((t, D_MODEL), BF16),
                   jax.ShapeDtypeStruct((t, ROUTER_PAD), F32)],
        compiler_params=_params(("parallel",)),
        name="merge",
    )(x2, h2d, *branches, *consts)


def _route(logits):
    lane = lax.broadcasted_iota(jnp.int32, logits.shape, 1).astype(F32)
    far = float(4 * LANES)
    is_group = lane < MOE_GROUPS
    gl = jnp.where(is_group, logits, NEG_BIG)
    gmax = jnp.max(gl, -1, keepdims=True)
    gidx = jnp.min(jnp.where(is_group & (gl == gmax), lane, far), -1, keepdims=True)
    gsum = jnp.sum(jnp.where(is_group, jnp.exp(gl - gmax), 0.0), -1, keepdims=True)
    first = MOE_GROUPS + MOE_PER_GROUP * gidx
    in_group = (lane >= first) & (lane < first + MOE_PER_GROUP)
    el = jnp.where(in_group, logits, NEG_BIG)
    m1 = jnp.max(el, -1, keepdims=True)
    i1 = jnp.min(jnp.where(in_group & (el == m1), lane, far), -1, keepdims=True)
    rest = in_group & (lane != i1)
    el2 = jnp.where(rest, logits, NEG_BIG)
    m2 = jnp.max(el2, -1, keepdims=True)
    i2 = jnp.min(jnp.where(rest & (el2 == m2), lane, far), -1, keepdims=True)
    e2 = jnp.exp(m2 - m1)
    w1 = 1.0 / (1.0 + e2)
    w2 = e2 / (1.0 + e2)
    return (jnp.where(lane == i1, w1, 0.0) + jnp.where(lane == i2, w2, 0.0)) / gsum


def _moe_kernel(h_ref, lg_ref, x1_ref, wgu_ref, wd_ref, fg_ref, o_ref, gate_ref, acc_ref, *, final_norm):
    e = pl.program_id(1)

    @pl.when(e == 0)
    def _():
        gate_ref[...] = _route(lg_ref[...])
        acc_ref[...] = jnp.zeros(acc_ref.shape, F32)

    lane = lax.broadcasted_iota(jnp.int32, gate_ref.shape, 1)
    gcol = jnp.sum(jnp.where(lane == e + MOE_GROUPS, gate_ref[...], 0.0), -1, keepdims=True)
    hgu = _dot(h_ref[...], wgu_ref[...])
    act = _silu(hgu[:, 0:MOE_HIDDEN]) * hgu[:, MOE_HIDDEN:2 * MOE_HIDDEN] * gcol
    acc_ref[...] += _dot(act.astype(BF16), wd_ref[...])

    @pl.when(e == pl.num_programs(1) - 1)
    def _():
        x2 = x1_ref[...] + acc_ref[...]
        if final_norm:
            x2 = x2 * lax.rsqrt(jnp.mean(x2 * x2, -1, keepdims=True) + NORM_EPS) * fg_ref[...]
        o_ref[...] = x2


def _moe(h2d, logits, x1, lw, final_g, final_norm):
    t = x1.shape[0]
    tm = min(t, 512)
    row = lambda w: pl.BlockSpec((tm, w), lambda i, e: (i, 0))
    return pl.pallas_call(
        functools.partial(_moe_kernel, final_norm=final_norm),
        grid=(t // tm, MOE_EXPERTS),
        in_specs=[row(D_MODEL), row(ROUTER_PAD), row(D_MODEL),
                  pl.BlockSpec((None, D_MODEL, 2 * MOE_HIDDEN), lambda i, e: (e, 0, 0)),
                  pl.BlockSpec((None, MOE_HIDDEN, D_MODEL), lambda i, e: (e, 0, 0)),
                  pl.BlockSpec((1, D_MODEL), lambda i, e: (0, 0))],
        out_specs=row(D_MODEL),
        out_shape=jax.ShapeDtypeStruct((t, D_MODEL), F32),
        scratch_shapes=[pltpu.VMEM((tm, ROUTER_PAD), F32), pltpu.VMEM((tm, D_MODEL), F32)],
        compiler_params=_params(("parallel", "arbitrary")),
        name="moe",
    )(h2d, logits, x1, lw['moe_w_gu'], lw['moe_w_down'], final_g)


def _prep_layer(params, l):
    g = lambda name: params[name][l]
    row = lambda a: a.reshape(1, -1).astype(F32)
    w_in = g('w_in')
    lora_rows = RWKV_PROJ - 3 * MIX

    def lora_pad(w, start):
        return jnp.zeros((lora_rows, MIX), F32).at[start:start + w.shape[0]].set(w)

    pw = g('pool_w')
    pool_wbd = jnp.zeros((MIX, MIX), F32)
    gw = MIX // len(POOL_WINDOWS)
    for gi in range(len(POOL_WINDOWS)):
        pool_wbd = pool_wbd.at[gi * gw:(gi + 1) * gw, gi * gw:(gi + 1) * gw].set(pw[gi])
    router_w = jnp.concatenate([g('moe_router_group_w'), g('moe_router_expert_w')], axis=1)
    router_b = jnp.concatenate([g('moe_router_group_b'), g('moe_router_expert_b')])
    n_r = router_w.shape[1]
    return {
        'norm_mix_g': row(g('norm_mix_g')),
        'w_mix': jnp.pad(w_in[:, :OFF_GATE], ((0, 0), (0, MIX_IN_PAD - OFF_GATE))).astype(BF16),
        'w_gate': w_in[:, OFF_GATE:].astype(BF16),
        'rwkv_mu': row(g('rwkv_mu')), 'rwkv_w0': row(g('rwkv_w0')), 'rwkv_a0': row(g('rwkv_a0')),
        'rwkv_kk': row(g('rwkv_kk')), 'rwkv_ka': row(g('rwkv_ka')),
        'rwkv_w2p': lora_pad(g('rwkv_w2'), 0),
        'rwkv_a2p': lora_pad(g('rwkv_a2'), RWKV_DECAY_LORA),
        'rwkv_g2p': lora_pad(g('rwkv_g2'), RWKV_DECAY_LORA + RWKV_AAA_LORA),
        'rwkv_rk': row(g('rwkv_rk')), 'rwkv_ln_g': row(g('rwkv_ln_g')), 'rwkv_ln_b': row(g('rwkv_ln_b')),
        'pool_wbd': pool_wbd.astype(BF16), 'pool_scale': row(g('pool_scale')),
        'conv_w': g('conv_w'), 'conv_b': row(g('conv_b')),
        'conv_ln_g': row(g('conv_ln_g')), 'conv_ln_b': row(g('conv_ln_b')),
        'ssm_conv_w': g('ssm_conv_w'), 'ssm_conv_b': row(g('ssm_conv_b')),
        'ssm_dt_bias_pad': jnp.pad(row(g('ssm_dt_bias')), ((0, 0), (0, LANES - SSM_HEADS))),
        'ssm_a_log_e': row(jnp.repeat(g('ssm_a_log'), SSM_HEAD_DIM)),
        'ssm_d_e': row(jnp.repeat(g('ssm_d'), SSM_HEAD_DIM)),
        'ssm_norm_g': row(g('ssm_norm_g')),
        'w_branch': g('w_branch').astype(BF16), 'w_out': g('w_out').astype(BF16),
        'norm_ffn_g': row(g('norm_ffn_g')),
        'router_w': jnp.pad(router_w, ((0, 0), (0, ROUTER_PAD - n_r))),
        'router_b': jnp.pad(row(router_b), ((0, 0), (0, ROUTER_PAD - n_r))),
        'moe_w_gu': jnp.concatenate([g('moe_w_gate'), g('moe_w_up')], axis=-1).astype(BF16),
        'moe_w_down': g('moe_w_down').astype(BF16),
    }


def _layer(x2, st, b, l, pos0, lw, final_g, final_norm):
    shift0, wkv0, pool0, conv0, sconv0, ssm0 = st
    h, p_rwkv, u_pool, s_conv, zx = _inproj(x2, lw['norm_mix_g'], lw['w_mix'])
    seq = lambda a: a.reshape(b, l, a.shape[-1])

    p3 = seq(p_rwkv)
    o_rwkv, wkv1t = _rwkv_scan(_rwkv_prep(p3, shift0, lw), lw, _wkv_to_pairs(wkv0))
    o_pool, pool1 = _pool(seq(u_pool), pool0, pos0, lw)
    o_conv, conv1 = _conv(seq(s_conv), conv0, lw)
    zx3 = seq(zx)
    xbc_raw = zx3[:, :, MIX:MIX + SSM_CONV_DIM]
    if l >= SSM_CONV_HIST:
        sconv1 = xbc_raw[:, l - SSM_CONV_HIST:]
    else:
        sconv1 = jnp.concatenate([sconv0, xbc_raw], axis=1)[:, -SSM_CONV_HIST:]
    lpad = -l % SUBLANES
    zx3p = jnp.pad(zx3, ((0, 0), (0, lpad), (0, 0))) if lpad else zx3
    o_ssm, ssm1t = _ssd(zx3p, sconv0, _ssm_to_rows(ssm0), l, lw)
    o_ssm = o_ssm[:, :l]

    flat = lambda a: a.reshape(b * l, MIX)
    x1, h2, logits = _merge(x2, h, [flat(o_rwkv), flat(o_pool), flat(o_conv), flat(o_ssm)], lw)
    x_out = _moe(h2, logits, x1, lw, final_g, final_norm)
    return x_out, (p3[:, -1], _wkv_from_pairs(wkv1t), pool1, conv1, sconv1, _ssm_from_rows(ssm1t))


def _trunk(x, states, pos0, layers, final_g):
    b, l, _ = x.shape
    x2 = x.reshape(b * l, D_MODEL)
    new = [[] for _ in states]
    for li, lw in enumerate(layers):
        st = tuple(s[li] for s in states)
        x2, st1 = _layer(x2, st, b, l, pos0, lw, final_g, li == len(layers) - 1)
        for lst, s in zip(new, st1):
            lst.append(s)
    return x2.reshape(b, l, D_MODEL), tuple(jnp.stack(lst) for lst in new)


def kernel(x_prompt, x_sample, state_rwkv_shift, state_rwkv_wkv, state_pool, state_conv, state_ssm_conv, state_ssm,
           norm_mix_g, w_in, rwkv_mu, rwkv_w0, rwkv_w2, rwkv_a0, rwkv_a2, rwkv_g2, rwkv_kk, rwkv_ka, rwkv_rk,
           rwkv_ln_g, rwkv_ln_b, pool_w, pool_scale, conv_w, conv_b, conv_ln_g, conv_ln_b, ssm_conv_w, ssm_conv_b,
           ssm_dt_bias, ssm_a_log, ssm_d, ssm_norm_g, w_branch, w_out, norm_ffn_g, moe_router_group_w,
           moe_router_group_b, moe_router_expert_w, moe_router_expert_b, moe_w_gate, moe_w_up, moe_w_down,
           final_norm_g):
    params = {
        'norm_mix_g': norm_mix_g, 'w_in': w_in, 'rwkv_mu': rwkv_mu, 'rwkv_w0': rwkv_w0, 'rwkv_w2': rwkv_w2,
        'rwkv_a0': rwkv_a0, 'rwkv_a2': rwkv_a2, 'rwkv_g2': rwkv_g2, 'rwkv_kk': rwkv_kk, 'rwkv_ka': rwkv_ka,
        'rwkv_rk': rwkv_rk, 'rwkv_ln_g': rwkv_ln_g, 'rwkv_ln_b': rwkv_ln_b, 'pool_w': pool_w,
        'pool_scale': pool_scale, 'conv_w': conv_w, 'conv_b': conv_b, 'conv_ln_g': conv_ln_g,
        'conv_ln_b': conv_ln_b, 'ssm_conv_w': ssm_conv_w, 'ssm_conv_b': ssm_conv_b, 'ssm_dt_bias': ssm_dt_bias,
        'ssm_a_log': ssm_a_log, 'ssm_d': ssm_d, 'ssm_norm_g': ssm_norm_g, 'w_branch': w_branch, 'w_out': w_out,
        'norm_ffn_g': norm_ffn_g, 'moe_router_group_w': moe_router_group_w,
        'moe_router_group_b': moe_router_group_b, 'moe_router_expert_w': moe_router_expert_w,
        'moe_router_expert_b': moe_router_expert_b, 'moe_w_gate': moe_w_gate, 'moe_w_up': moe_w_up,
        'moe_w_down': moe_w_down,
    }
    layers = [_prep_layer(params, l) for l in range(DEPTH)]
    final_g = final_norm_g.reshape(1, D_MODEL).astype(F32)
    bp, dtp = x_prompt.shape[0], x_prompt.dtype
    empty = (jnp.zeros((DEPTH, bp, RWKV_PROJ), dtp),
             jnp.zeros((DEPTH, bp, RWKV_HEADS, RWKV_HEAD, RWKV_HEAD), dtp),
             jnp.zeros((DEPTH, bp, POOL_HIST, MIX), dtp),
             jnp.zeros((DEPTH, bp, CONV_HIST, MIX), dtp),
             jnp.zeros((DEPTH, bp, SSM_CONV_HIST, SSM_CONV_DIM), dtp),
             jnp.zeros((DEPTH, bp, SSM_HEADS, SSM_HEAD_DIM, SSM_STATE), dtp))
    y_prompt, p_states = _trunk(x_prompt, empty, 0, layers, final_g)
    past = (state_rwkv_shift, state_rwkv_wkv, state_pool, state_conv, state_ssm_conv, state_ssm)
    y_sample, s_states = _trunk(x_sample, past, PAST_LEN, layers, final_g)
    return (y_prompt, y_sample) + tuple(p_states) + tuple(s_states)
```

```python
import functools
import math

import jax
import jax.numpy as jnp
from jax import lax
from jax.experimental import pallas as pl
from jax.experimental.pallas import tpu as pltpu

F32 = jnp.float32
BF16 = jnp.bfloat16
HIGHEST = lax.Precision.HIGHEST

D_MODEL = 1024
DEPTH = 2
PAST_LEN = 16384
MIX = D_MODEL // 4
N_BRANCH = 4
RWKV_HEAD = 64
RWKV_HEADS = MIX // RWKV_HEAD
RWKV_DECAY_LORA = 32
RWKV_AAA_LORA = 32
RWKV_GATE_LORA = 64
RWKV_PROJ = 3 * MIX + RWKV_DECAY_LORA + RWKV_AAA_LORA + RWKV_GATE_LORA
RWKV_GN_EPS = 64e-5
POOL_WINDOWS = (2, 4, 8, 16)
POOL_HIST = 15
CONV_KERNEL = 31
CONV_HIST = CONV_KERNEL - 1
CONV_LN_EPS = 1e-5
SSM_HEAD_DIM = 64
SSM_HEADS = MIX // SSM_HEAD_DIM
SSM_GROUPS = 2
SSM_STATE = 128
SSM_CONV_KERNEL = 4
SSM_CONV_HIST = SSM_CONV_KERNEL - 1
SSM_CONV_DIM = MIX + 2 * SSM_GROUPS * SSM_STATE
SSM_PROJ = MIX + SSM_CONV_DIM + SSM_HEADS
OFF_POOL = RWKV_PROJ
OFF_CONV = OFF_POOL + MIX
OFF_SSM = OFF_CONV + 2 * MIX
OFF_GATE = OFF_SSM + SSM_PROJ
MOE_GROUPS = 4
MOE_PER_GROUP = 4
MOE_EXPERTS = MOE_GROUPS * MOE_PER_GROUP
MOE_HIDDEN = 256
NORM_EPS = 1e-6

LANES = 128
SUBLANES = 8
VMEM_LIMIT_BYTES = 48 * 1024 * 1024

SSM_IN_PAD = 9 * LANES
MIX_IN_PAD = OFF_SSM + SSM_IN_PAD
ROUTER_PAD = LANES
IN_PROJ = OFF_GATE + N_BRANCH * D_MODEL
GATE_COL0 = OFF_GATE // LANES * LANES
GATE_SHIFT = OFF_GATE - GATE_COL0
GATE_SLAB = -(-(IN_PROJ - GATE_COL0) // LANES) * LANES
GATE_WIN = D_MODEL + LANES
SSD_CHUNK = 128
RWKV_CHUNK = 16
RWKV_BATCH_BLOCK = 8
NEG_BIG = -1e30


def _sigmoid(x):
    return 1.0 / (1.0 + jnp.exp(-x))


def _silu(x):
    return x * _sigmoid(x)


def _softplus(x):
    return jnp.maximum(x, 0.0) + jnp.log(1.0 + jnp.exp(-jnp.abs(x)))


def _dot(a, b):
    return jnp.dot(a, b, preferred_element_type=F32)


def _dot_hi(a, b):
    return jnp.dot(a, b, preferred_element_type=F32, precision=HIGHEST)


def _dot_nt(a, b, precision=None):
    return lax.dot_general(a, b, (((1,), (1,)), ((), ())), preferred_element_type=F32, precision=precision)


def _eye(n):
    return (lax.broadcasted_iota(jnp.int32, (n, n), 0) == lax.broadcasted_iota(jnp.int32, (n, n), 1)).astype(F32)


def _head_ones(n, head):
    r = lax.broadcasted_iota(jnp.int32, (n, n), 0) // head
    c = lax.broadcasted_iota(jnp.int32, (n, n), 1) // head
    return (r == c).astype(F32)


def _params(sem):
    return pltpu.CompilerParams(dimension_semantics=sem, vmem_limit_bytes=VMEM_LIMIT_BYTES)


def _const_spec(shape):
    nd = len(shape)
    return pl.BlockSpec(shape, lambda *_: (0,) * nd)


def _cast_w_in_kernel(w_ref, mix_ref, gate_ref):
    rows = w_ref.shape[0]
    lane = lax.broadcasted_iota(jnp.int32, (rows, LANES), 1)
    mix_ref[:, 0:GATE_COL0] = w_ref[:, 0:GATE_COL0].astype(BF16)
    tail = jnp.where(lane < GATE_SHIFT, w_ref[:, GATE_COL0:GATE_COL0 + LANES], 0.0)
    mix_ref[:, GATE_COL0:MIX_IN_PAD] = tail.astype(BF16)
    whole = (IN_PROJ - GATE_COL0) // LANES * LANES
    gate_ref[:, 0:whole] = w_ref[:, GATE_COL0:GATE_COL0 + whole].astype(BF16)
    gate_ref[:, whole:GATE_SLAB] = jnp.zeros((rows, GATE_SLAB - whole), BF16)
    gate_ref[:, whole:IN_PROJ - GATE_COL0] = w_ref[:, GATE_COL0 + whole:IN_PROJ].astype(BF16)


def _cast_w_in(w_in, l):
    tr = 128
    return pl.pallas_call(
        _cast_w_in_kernel,
        grid=(D_MODEL // tr,),
        in_specs=[pl.BlockSpec((None, tr, IN_PROJ), lambda i: (l, i, 0))],
        out_specs=[pl.BlockSpec((tr, MIX_IN_PAD), lambda i: (i, 0)), pl.BlockSpec((tr, GATE_SLAB), lambda i: (i, 0))],
        out_shape=[jax.ShapeDtypeStruct((D_MODEL, MIX_IN_PAD), BF16),
                   jax.ShapeDtypeStruct((D_MODEL, GATE_SLAB), BF16)],
        compiler_params=_params(("parallel",)),
        name="cast_w_in",
    )(w_in)


def _inproj_kernel(x_ref, g_ref, w_ref, h_ref, rw_ref, pool_ref, conv_ref, ssm_ref):
    x = x_ref[...]
    y = x * lax.rsqrt(jnp.mean(x * x, -1, keepdims=True) + NORM_EPS) * g_ref[...]
    hb = y.astype(BF16)
    h_ref[...] = hb
    rw_ref[...] = _dot(hb, w_ref[:, 0:OFF_POOL])
    pool_ref[...] = _dot(hb, w_ref[:, OFF_POOL:OFF_CONV])
    conv_ref[...] = _dot(hb, w_ref[:, OFF_CONV:OFF_SSM])
    ssm_ref[...] = _dot(hb, w_ref[:, OFF_SSM:MIX_IN_PAD])


def _inproj(x2, g, w_mix):
    t = x2.shape[0]
    tm = min(t, 512)
    row = lambda w: pl.BlockSpec((tm, w), lambda i: (i, 0))
    return pl.pallas_call(
        _inproj_kernel,
        grid=(t // tm,),
        in_specs=[row(D_MODEL), _const_spec((1, D_MODEL)), _const_spec((D_MODEL, MIX_IN_PAD))],
        out_specs=[row(D_MODEL), row(RWKV_PROJ), row(MIX), row(2 * MIX), row(SSM_IN_PAD)],
        out_shape=[jax.ShapeDtypeStruct((t, D_MODEL), BF16),
                   jax.ShapeDtypeStruct((t, RWKV_PROJ), F32),
                   jax.ShapeDtypeStruct((t, MIX), F32),
                   jax.ShapeDtypeStruct((t, 2 * MIX), F32),
                   jax.ShapeDtypeStruct((t, SSM_IN_PAD), F32)],
        compiler_params=_params(("parallel",)),
        name="inproj",
    )(x2, g, w_mix)


def _rwkv_prep_math(p, prev, mu, w0, a0, kkw, kaw, w2p, a2p, g2p):
    q = p + (prev - p) * mu
    r = q[:, 0:MIX]
    k = q[:, MIX:2 * MIX]
    v = q[:, 2 * MIX:3 * MIX]
    lora = q[:, 3 * MIX:RWKV_PROJ]
    zw = w0 + _dot_hi(jnp.tanh(lora), w2p)
    decay = jnp.exp(-math.exp(-0.5) * _sigmoid(zw))
    a = _sigmoid(a0 + _dot_hi(lora, a2p))
    g = _dot_hi(_sigmoid(lora), g2p)
    kk = k * kkw
    ss = _dot_hi(kk * kk, _head_ones(MIX, RWKV_HEAD))
    kk = kk * lax.rsqrt(jnp.maximum(ss, 1e-24))
    k2 = k * (1.0 + (a - 1.0) * kaw)
    return r, k2, v, kk, kk * a, decay, g


def _rwkv_prep_seq_kernel(p_ref, s0_ref, mu_ref, w0_ref, a0_ref, kkw_ref, kaw_ref, w2_ref, a2_ref, g2_ref,
                          r_ref, k_ref, v_ref, kk_ref, b_ref, d_ref, g_ref, carry_ref):
    @pl.when(pl.program_id(1) == 0)
    def _():
        carry_ref[...] = s0_ref[...]

    p = p_ref[...]
    rows = p.shape[0]
    first = lax.broadcasted_iota(jnp.int32, p.shape, 0) == 0
    prev = jnp.where(first, carry_ref[...], pltpu.roll(p, 1, 0))
    carry_ref[...] = p[rows - 1:rows, :]
    outs = _rwkv_prep_math(p, prev, mu_ref[...], w0_ref[...], a0_ref[...], kkw_ref[...], kaw_ref[...],
                           w2_ref[...], a2_ref[...], g2_ref[...])
    for ref, val in zip((r_ref, k_ref, v_ref, kk_ref, b_ref, d_ref, g_ref), outs):
        ref[...] = val


def _rwkv_prep_tok_kernel(p_ref, prev_ref, mu_ref, w0_ref, a0_ref, kkw_ref, kaw_ref, w2_ref, a2_ref, g2_ref,
                          r_ref, k_ref, v_ref, kk_ref, b_ref, d_ref, g_ref):
    outs = _rwkv_prep_math(p_ref[...], prev_ref[...], mu_ref[...], w0_ref[...], a0_ref[...], kkw_ref[...],
                           kaw_ref[...], w2_ref[...], a2_ref[...], g2_ref[...])
    for ref, val in zip((r_ref, k_ref, v_ref, kk_ref, b_ref, d_ref, g_ref), outs):
        ref[...] = val


def _rwkv_prep(p3, shift0, lw):
    b, l, _ = p3.shape
    consts = [lw['rwkv_mu'], lw['rwkv_w0'], lw['rwkv_a0'], lw['rwkv_kk'], lw['rwkv_ka'],
              lw['rwkv_w2p'], lw['rwkv_a2p'], lw['rwkv_g2p']]
    const_specs = [_const_spec(c.shape) for c in consts]
    if l == 1:
        outs = pl.pallas_call(
            _rwkv_prep_tok_kernel,
            grid=(1,),
            in_specs=[_const_spec((b, RWKV_PROJ)), _const_spec((b, RWKV_PROJ))] + const_specs,
            out_specs=[_const_spec((b, MIX))] * 7,
            out_shape=[jax.ShapeDtypeStruct((b, MIX), F32)] * 7,
            compiler_params=_params(("arbitrary",)),
            name="rwkv_prep_tok",
        )(p3[:, 0], shift0, *consts)
        return [o[:, None, :] for o in outs]
    lt = min(l, 512)
    seq = lambda w: pl.BlockSpec((None, lt, w), lambda i, j: (i, j, 0))
    return pl.pallas_call(
        _rwkv_prep_seq_kernel,
        grid=(b, l // lt),
        in_specs=[seq(RWKV_PROJ), pl.BlockSpec((None, 1, RWKV_PROJ), lambda i, j: (i, 0, 0))] + const_specs,
        out_specs=[seq(MIX)] * 7,
        out_shape=[jax.ShapeDtypeStruct((b, l, MIX), F32)] * 7,
        scratch_shapes=[pltpu.VMEM((1, RWKV_PROJ), F32)],
        compiler_params=_params(("parallel", "arbitrary")),
        name="rwkv_prep_seq",
    )(p3, shift0[:, None, :], *consts)


def _rwkv_scan_kernel(r_ref, k_ref, v_ref, kk_ref, b_ref, d_ref, g_ref, lng_ref, lnb_ref, rk_ref, s0_ref,
                      o_ref, s1_ref, st_ref, col_ref, row_ref, y_ref, *, nb, lc):
    c = pl.program_id(2)
    n = RWKV_HEAD

    @pl.when(c == 0)
    def _():
        for bb in range(nb):
            st_ref[bb] = _dot_nt(_eye(n), s0_ref[bb], HIGHEST)

    ones = _head_ones(LANES, n).astype(BF16)
    ones2 = jnp.concatenate([ones, ones], axis=0)
    lane = lax.broadcasted_iota(jnp.int32, (n, LANES), 1)
    sub = lax.broadcasted_iota(jnp.int32, (n, LANES), 0)
    diag = (lane % n == sub).astype(BF16)
    low = lax.broadcasted_iota(jnp.int32, (lc, LANES), 1) < n

    def head_sum(x):
        s_lo = jnp.sum(jnp.where(low, x, 0.0), -1, keepdims=True)
        s_hi = jnp.sum(jnp.where(low, 0.0, x), -1, keepdims=True)
        return jnp.where(low, s_lo, s_hi)

    def spread2(x):
        hi = x.astype(BF16)
        lo = (x - hi.astype(F32)).astype(BF16)
        parts = [(p[:, None, :] * diag[None, :, :]).reshape(lc * n, LANES) for p in (hi, lo)]
        return jnp.concatenate(parts, axis=-1)

    for bb in range(nb):
        r = r_ref[bb]
        k = k_ref[bb]
        kk = kk_ref[bb]
        bv = b_ref[bb]
        d = d_ref[bb]
        for idx, x in enumerate((kk, d, bv, k, d * r)):
            col_ref[idx, bb] = _dot(spread2(x), ones2)
        row_ref[0, bb] = head_sum(bv * r)
        row_ref[1, bb] = head_sum(k * r)

    def step(t, carry):
        base = pl.multiple_of(t * n, n)
        for bb in range(nb):
            st = st_ref[bb]
            ckk = col_ref[0, bb, pl.ds(base, n), :]
            cd = col_ref[1, bb, pl.ds(base, n), :]
            cb = col_ref[2, bb, pl.ds(base, n), :]
            ck = col_ref[3, bb, pl.ds(base, n), :]
            cdr = col_ref[4, bb, pl.ds(base, n), :]
            vrow = v_ref[bb, pl.ds(t, 1), :]
            br = row_ref[0, bb, pl.ds(t, 1), :]
            kr = row_ref[1, bb, pl.ds(t, 1), :]
            sa = -jnp.sum(st * ckk, axis=0, keepdims=True)
            y = jnp.sum(st * cdr, axis=0, keepdims=True) + sa * br + vrow * kr
            st_ref[bb] = st * cd + sa * cb + vrow * ck
            y_ref[bb, pl.ds(t, 1), :] = y
        return carry

    lax.fori_loop(0, lc, step, 0)

    lng = lng_ref[...]
    lnb = lnb_ref[...]
    rk = rk_ref[...]
    inv_n = 1.0 / n
    for bb in range(nb):
        y = y_ref[bb]
        mu = head_sum(y) * inv_n
        yc = y - mu
        var = head_sum(yc * yc) * inv_n
        yn = yc * lax.rsqrt(var + RWKV_GN_EPS) * lng + lnb
        r = r_ref[bb]
        v = v_ref[bb]
        rkv = head_sum(r * k_ref[bb] * rk)
        o_ref[bb] = (yn + rkv * v) * g_ref[bb]

    @pl.when(c == pl.num_programs(2) - 1)
    def _():
        for bb in range(nb):
            s1_ref[bb] = _dot_nt(_eye(LANES), st_ref[bb], HIGHEST)


def _rwkv_scan(prep, lw, s0t):
    r, k, v, kk, bv, d, g = prep
    b, l, _ = r.shape
    nb = RWKV_BATCH_BLOCK
    lc = min(l, RWKV_CHUNK)
    n = RWKV_HEAD
    seq = pl.BlockSpec((nb, lc, LANES), lambda i, p, c: (i, c, p))
    vec = pl.BlockSpec((1, LANES), lambda i, p, c: (0, p))
    st = pl.BlockSpec((nb, None, LANES, n), lambda i, p, c: (i, p, 0, 0))
    kern = functools.partial(_rwkv_scan_kernel, nb=nb, lc=lc)
    return pl.pallas_call(
        kern,
        grid=(b // nb, 2, l // lc),
        in_specs=[seq] * 7 + [vec] * 3 + [st],
        out_specs=[seq, st],
        out_shape=[jax.ShapeDtypeStruct((b, l, MIX), F32), jax.ShapeDtypeStruct((b, 2, LANES, n), F32)],
        scratch_shapes=[pltpu.VMEM((nb, n, LANES), F32),
                        pltpu.VMEM((5, nb, lc * n, LANES), F32),
                        pltpu.VMEM((2, nb, lc, LANES), F32),
                        pltpu.VMEM((nb, lc, LANES), F32)],
        compiler_params=_params(("parallel", "parallel", "arbitrary")),
        name="rwkv_scan",
    )(r, k, v, kk, bv, d, g, lw['rwkv_ln_g'], lw['rwkv_ln_b'], lw['rwkv_rk'], s0t)


def _wkv_to_pairs(s):
    return s.reshape(s.shape[0], 2, 2 * RWKV_HEAD, RWKV_HEAD)


def _wkv_from_pairs(s):
    return s.reshape(s.shape[0], RWKV_HEADS, RWKV_HEAD, RWKV_HEAD)


def _pool_math(tap, pos, pw, scale):
    rows = pos.shape[0]
    lane = lax.broadcasted_iota(jnp.int32, (rows, LANES), 1)
    low = lane < (LANES // 2)
    cnt = lambda w: jnp.minimum(pos + 1, w).astype(F32)
    a0 = tap(0, 0)
    s2 = a0 + tap(1, 0)
    s4 = s2 + tap(2, 0) + tap(3, 0)
    b0 = tap(0, 1)
    s8 = b0
    for k in range(1, 8):
        s8 = s8 + tap(k, 1)
    s16 = s8
    for k in range(8, 16):
        s16 = s16 + tap(k, 1)
    diff_a = jnp.where(low, s2 / cnt(2), s4 / cnt(4)) - a0
    diff_b = jnp.where(low, s8 / cnt(8), s16 / cnt(16)) - b0
    diff = jnp.concatenate([diff_a, diff_b], axis=-1).astype(BF16)
    return _dot(diff, pw) * scale


def _pool_seq_kernel(u_ref, h_ref, pw_ref, sc_ref, o_ref, h1_ref, full_ref, *, l, tt):
    pad = POOL_HIST + 1
    full_ref[0:1, :] = jnp.zeros((1, MIX), F32)
    full_ref[1:pad, :] = h_ref[...]
    full_ref[pad:pad + l, :] = u_ref[...]
    pw = pw_ref[...]
    sc = sc_ref[...]
    for i in range(l // tt):
        t0 = i * tt
        tap = lambda k, half: full_ref[pad + t0 - k:pad + t0 - k + tt, half * LANES:(half + 1) * LANES]
        pos = t0 + lax.broadcasted_iota(jnp.int32, (tt, 1), 0)
        o_ref[t0:t0 + tt, :] = _pool_math(tap, pos, pw, sc)
    h1_ref[...] = full_ref[l + 1:l + pad, :]


def _pool_tok_kernel(u_ref, ht_ref, pw_ref, sc_ref, o_ref, *, pos0):
    def tap(k, half):
        sl = slice(half * LANES, (half + 1) * LANES)
        return u_ref[:, sl] if k == 0 else ht_ref[POOL_HIST - k, :, sl]
    rows = u_ref.shape[0]
    pos = jnp.full((rows, 1), pos0, jnp.int32)
    o_ref[...] = _pool_math(tap, pos, pw_ref[...], sc_ref[...])


def _pool(u3, hist, pos0, lw):
    b, l, _ = u3.shape
    consts = [lw['pool_wbd'], lw['pool_scale']]
    const_specs = [_const_spec(c.shape) for c in consts]
    if l == 1:
        ht = jnp.swapaxes(hist, 0, 1)
        out = pl.pallas_call(
            functools.partial(_pool_tok_kernel, pos0=pos0),
            grid=(1,),
            in_specs=[_const_spec((b, MIX)), _const_spec((POOL_HIST, b, MIX))] + const_specs,
            out_specs=_const_spec((b, MIX)),
            out_shape=jax.ShapeDtypeStruct((b, MIX), F32),
            compiler_params=_params(("arbitrary",)),
            name="pool_tok",
        )(u3[:, 0], ht, *consts)
        return out[:, None, :], jnp.concatenate([hist[:, 1:], u3], axis=1)
    assert pos0 == 0 and l >= POOL_HIST
    tt = min(l, 256)
    seq = pl.BlockSpec((None, l, MIX), lambda i: (i, 0, 0))
    hs = pl.BlockSpec((None, POOL_HIST, MIX), lambda i: (i, 0, 0))
    return pl.pallas_call(
        functools.partial(_pool_seq_kernel, l=l, tt=tt),
        grid=(b,),
        in_specs=[seq, hs] + const_specs,
        out_specs=[seq, hs],
        out_shape=[jax.ShapeDtypeStruct((b, l, MIX), F32), jax.ShapeDtypeStruct((b, POOL_HIST, MIX), F32)],
        scratch_shapes=[pltpu.VMEM((l + POOL_HIST + 1, MIX), F32)],
        compiler_params=_params(("parallel",)),
        name="pool_seq",
    )(u3, hist, *consts)


def _conv_post(y, cb, g, beta):
    y = y + cb
    mu = jnp.mean(y, -1, keepdims=True)
    yc = y - mu
    var = jnp.mean(yc * yc, -1, keepdims=True)
    return _silu(yc * lax.rsqrt(var + CONV_LN_EPS) * g + beta)


def _glu(s):
    return s[:, 0:MIX] * _sigmoid(s[:, MIX:2 * MIX])


def _conv_seq_kernel(s_ref, h_ref, w_ref, cb_ref, g_ref, beta_ref, o_ref, h1_ref, full_ref, *, l, tt):
    pad = CONV_HIST + 2
    full_ref[0:2, :] = jnp.zeros((2, MIX), F32)
    full_ref[2:pad, :] = h_ref[...]
    full_ref[pad:pad + l, :] = _glu(s_ref[...])
    cb = cb_ref[...]
    g = g_ref[...]
    beta = beta_ref[...]

    for i in range(l // tt):
        t0 = i * tt
        acc = jnp.zeros((tt, MIX), F32)
        for k in range(CONV_KERNEL):
            acc = acc + w_ref[k:k + 1, :] * full_ref[t0 + k + 2:t0 + k + 2 + tt, :]
        o_ref[t0:t0 + tt, :] = _conv_post(acc, cb, g, beta)
    h1_ref[...] = full_ref[l + 2:l + pad, :]


def _conv_tok_kernel(s_ref, ht_ref, w_ref, cb_ref, g_ref, beta_ref, o_ref, c_ref):
    c = _glu(s_ref[...])
    c_ref[...] = c
    acc = w_ref[CONV_HIST:CONV_KERNEL, :] * c
    for k in range(CONV_HIST):
        acc = acc + w_ref[k:k + 1, :] * ht_ref[k]
    o_ref[...] = _conv_post(acc, cb_ref[...], g_ref[...], beta_ref[...])


def _conv(s3, hist, lw):
    b, l, _ = s3.shape
    consts = [lw['conv_w'], lw['conv_b'], lw['conv_ln_g'], lw['conv_ln_b']]
    const_specs = [_const_spec(c.shape) for c in consts]
    if l == 1:
        ht = jnp.swapaxes(hist, 0, 1)
        out, c = pl.pallas_call(
            _conv_tok_kernel,
            grid=(1,),
            in_specs=[_const_spec((b, 2 * MIX)), _const_spec((CONV_HIST, b, MIX))] + const_specs,
            out_specs=[_const_spec((b, MIX))] * 2,
            out_shape=[jax.ShapeDtypeStruct((b, MIX), F32)] * 2,
            compiler_params=_params(("arbitrary",)),
            name="conv_tok",
        )(s3[:, 0], ht, *consts)
        return out[:, None, :], jnp.concatenate([hist[:, 1:], c[:, None, :]], axis=1)
    assert l >= CONV_HIST
    tt = min(l, 64)
    hs = pl.BlockSpec((None, CONV_HIST, MIX), lambda i: (i, 0, 0))
    return pl.pallas_call(
        functools.partial(_conv_seq_kernel, l=l, tt=tt),
        grid=(b,),
        in_specs=[pl.BlockSpec((None, l, 2 * MIX), lambda i: (i, 0, 0)), hs] + const_specs,
        out_specs=[pl.BlockSpec((None, l, MIX), lambda i: (i, 0, 0)), hs],
        out_shape=[jax.ShapeDtypeStruct((b, l, MIX), F32), jax.ShapeDtypeStruct((b, CONV_HIST, MIX), F32)],
        scratch_shapes=[pltpu.VMEM((l + CONV_HIST + 2, MIX), F32)],
        compiler_params=_params(("parallel",)),
        name="conv_seq",
    )(s3, hist, *consts)


def _ssd_kernel(zx_ref, ch_ref, h0_ref, cw_ref, cb_ref, dtb_ref, alog_ref, dd_ref, ng_ref,
                o_ref, h1_ref, xf_ref, ht_ref, *, lb, lvalid):
    q = SSD_CHUNK
    c = pl.program_id(1)
    halo = SUBLANES
    gw = SSM_GROUPS * SSM_STATE

    @pl.when(c == 0)
    def _():
        ht_ref[...] = h0_ref[...]
        xf_ref[...] = jnp.zeros(xf_ref.shape, F32)
        xf_ref[halo - SSM_CONV_HIST:halo, :] = ch_ref[...]

    xf_ref[halo:halo + lb, :] = zx_ref[:, MIX:MIX + SSM_CONV_DIM]
    acc = jnp.zeros((q, SSM_CONV_DIM), F32)
    for k in range(SSM_CONV_KERNEL):
        acc = acc + cw_ref[k:k + 1, :] * xf_ref[halo - SSM_CONV_HIST + k:halo - SSM_CONV_HIST + k + q, :]
    xbc = _silu(acc + cb_ref[...])
    xf_ref[halo - SSM_CONV_HIST:halo, :] = xf_ref[halo + q - SSM_CONV_HIST:halo + q, :]

    xs = xbc[:, 0:MIX]
    bm = xbc[:, MIX:MIX + gw]
    cm = xbc[:, MIX + gw:MIX + 2 * gw]

    hsel = (lax.broadcasted_iota(jnp.int32, (LANES, MIX), 0)
            == lax.broadcasted_iota(jnp.int32, (LANES, MIX), 1) // SSM_HEAD_DIM).astype(F32)
    dt_raw = zx_ref[:, MIX + SSM_CONV_DIM:SSM_IN_PAD]
    if lb < q:
        dt_raw = jnp.concatenate([dt_raw, jnp.zeros((q - lb, LANES), F32)], axis=0)
    trow = lax.broadcasted_iota(jnp.int32, (q, 1), 0) + c * q
    dt = jnp.where(trow < lvalid, _softplus(dt_raw + dtb_ref[...]), 0.0)
    dte = _dot_hi(dt, hsel)
    dae = dte * (-jnp.exp(alog_ref[...]))
    ti = lax.broadcasted_iota(jnp.int32, (q, q), 0)
    tj = lax.broadcasted_iota(jnp.int32, (q, q), 1)
    causal = ti >= tj
    acse = _dot_hi(causal.astype(F32), dae)
    xdt = xs * dte
    last = acse[q - 1:q, :]
    w_end = xdt * jnp.exp(last - acse)
    eacs = jnp.exp(acse)

    lane_g = lax.broadcasted_iota(jnp.int32, (q, LANES), 1)
    heads_per_group = SSM_HEADS // SSM_GROUPS
    ys = []
    for g in range(SSM_GROUPS):
        lo = g * LANES
        bg = bm[:, g * SSM_STATE:(g + 1) * SSM_STATE].astype(BF16)
        cg = cm[:, g * SSM_STATE:(g + 1) * SSM_STATE].astype(BF16)
        gmat = _dot_nt(cg, bg)
        hg = ht_ref[lo:lo + LANES, :]
        yg = _dot_nt(cg, hg.astype(BF16)) * eacs[:, lo:lo + LANES]
        xg = xdt[:, lo:lo + LANES]
        for hh in range(heads_per_group):
            col = acse[:, lo + hh * SSM_HEAD_DIM:lo + hh * SSM_HEAD_DIM + 1]
            row = jnp.transpose(jnp.broadcast_to(col, (q, q)))
            seg = jnp.where(causal, col - row, NEG_BIG)
            scores = (gmat * jnp.exp(seg)).astype(BF16)
            xm = jnp.where(lane_g // SSM_HEAD_DIM == hh, xg, 0.0).astype(BF16)
            yg = yg + _dot(scores, xm)
        ys.append(yg)
        st = _dot(jnp.transpose(w_end[:, lo:lo + LANES]).astype(BF16), bg)
        for hh in range(heads_per_group):
            r0 = hh * SSM_HEAD_DIM
            decay = jnp.exp(last[:, lo + r0:lo + r0 + 1])
            ht_ref[lo + r0:lo + r0 + SSM_HEAD_DIM, :] = hg[r0:r0 + SSM_HEAD_DIM] * decay + st[r0:r0 + SSM_HEAD_DIM]
    y = jnp.concatenate(ys, axis=-1)

    y = y + dd_ref[...] * xs
    z = zx_ref[:, 0:MIX]
    if lb < q:
        z = jnp.concatenate([z, jnp.zeros((q - lb, MIX), F32)], axis=0)
    y = y * _silu(z)
    y = y * lax.rsqrt(jnp.mean(y * y, -1, keepdims=True) + NORM_EPS) * ng_ref[...]
    o_ref[...] = y[0:lb, :]

    @pl.when(c == pl.num_programs(1) - 1)
    def _():
        h1_ref[...] = ht_ref[...]


def _ssd(zx3, conv_hist, h0t, lvalid, lw):
    b, lp, _ = zx3.shape
    q = SSD_CHUNK
    lb = min(lp, q)
    consts = [lw['ssm_conv_w'], lw['ssm_conv_b'], lw['ssm_dt_bias_pad'], lw['ssm_a_log_e'], lw['ssm_d_e'],
              lw['ssm_norm_g']]
    const_specs = [pl.BlockSpec(cst.shape, lambda i, c: (0, 0)) for cst in consts]
    hs = pl.BlockSpec((None, MIX, SSM_STATE), lambda i, c: (i, 0, 0))
    return pl.pallas_call(
        functools.partial(_ssd_kernel, lb=lb, lvalid=lvalid),
        grid=(b, lp // lb),
        in_specs=[pl.BlockSpec((None, lb, SSM_IN_PAD), lambda i, c: (i, c, 0)),
                  pl.BlockSpec((None, SSM_CONV_HIST, SSM_CONV_DIM), lambda i, c: (i, 0, 0)), hs] + const_specs,
        out_specs=[pl.BlockSpec((None, lb, MIX), lambda i, c: (i, c, 0)), hs],
        out_shape=[jax.ShapeDtypeStruct((b, lp, MIX), F32), jax.ShapeDtypeStruct((b, MIX, SSM_STATE), F32)],
        scratch_shapes=[pltpu.VMEM((SUBLANES + q, SSM_CONV_DIM), F32), pltpu.VMEM((MIX, SSM_STATE), F32)],
        compiler_params=_params(("parallel", "arbitrary")),
        name="ssd",
    )(zx3, conv_hist, h0t, *consts)


def _ssm_to_rows(s):
    return s.reshape(s.shape[0], MIX, SSM_STATE)


def _ssm_from_rows(s):
    return s.reshape(s.shape[0], SSM_HEADS, SSM_HEAD_DIM, SSM_STATE)


def _merge_kernel(x_ref, h_ref, b0_ref, b1_ref, b2_ref, b3_ref, wg_ref, wb_ref, wo_ref, ng_ref, wr_ref, br_ref,
                  x1_ref, h2_ref, lg_ref):
    h = h_ref[...]
    merged = None
    for kbr, b_ref in enumerate((b0_ref, b1_ref, b2_ref, b3_ref)):
        gate = _sigmoid(_dot(h, wg_ref[:, kbr * D_MODEL:kbr * D_MODEL + GATE_WIN]))
        up = _dot(b_ref[...].astype(BF16), wb_ref[kbr])
        merged = up * gate if merged is None else merged + up * gate
    x1 = x_ref[...] + _dot(merged.astype(BF16), wo_ref[...])
    x1_ref[...] = x1
    hn = x1 * lax.rsqrt(jnp.mean(x1 * x1, -1, keepdims=True) + NORM_EPS) * ng_ref[...]
    h2_ref[...] = hn.astype(BF16)
    lg_ref[...] = _dot_hi(hn, wr_ref[...]) + br_ref[...]


def _merge(x2, h2d, branches, lw):
    t = x2.shape[0]
    tm = min(t, 256)
    row = lambda w: pl.BlockSpec((tm, w), lambda i: (i, 0))
    consts = [lw['w_gate'], lw['w_branch'], lw['w_out'], lw['norm_ffn_g'], lw['router_w'], lw['router_b']]
    return pl.pallas_call(
        _merge_kernel,
        grid=(t // tm,),
        in_specs=[row(D_MODEL), row(D_MODEL)] + [row(MIX)] * 4 + [_const_spec(c.shape) for c in consts],
        out_specs=[row(D_MODEL), row(D_MODEL), row(ROUTER_PAD)],
        out_shape=[jax.ShapeDtypeStruct((t, D_MODEL), F32), jax.ShapeDtypeStruct((t, D_MODEL), BF16),
                   jax.ShapeDtypeStruct((t, ROUTER_PAD), F32)],
        compiler_params=_params(("parallel",)),
        name="merge",
    )(x2, h2d, *branches, *consts)


def _route(logits):
    lane = lax.broadcasted_iota(jnp.int32, logits.shape, 1).astype(F32)
    far = float(4 * LANES)
    is_group = lane < MOE_GROUPS
    gl = jnp.where(is_group, logits, NEG_BIG)
    gmax = jnp.max(gl, -1, keepdims=True)
    gidx = jnp.min(jnp.where(is_group & (gl == gmax), lane, far), -1, keepdims=True)
    gsum = jnp.sum(jnp.where(is_group, jnp.exp(gl - gmax), 0.0), -1, keepdims=True)
    first = MOE_GROUPS + MOE_PER_GROUP * gidx
    in_group = (lane >= first) & (lane < first + MOE_PER_GROUP)
    el = jnp.where(in_group, logits, NEG_BIG)
    m1 = jnp.max(el, -1, keepdims=True)
    i1 = jnp.min(jnp.where(in_group & (el == m1), lane, far), -1, keepdims=True)
    rest = in_group & (lane != i1)
    el2 = jnp.where(rest, logits, NEG_BIG)
    m2 = jnp.max(el2, -1, keepdims=True)
    i2 = jnp.min(jnp.where(rest & (el2 == m2), lane, far), -1, keepdims=True)
    e2 = jnp.exp(m2 - m1)
    w1 = 1.0 / (1.0 + e2)
    w2 = e2 / (1.0 + e2)
    return (jnp.where(lane == i1, w1, 0.0) + jnp.where(lane == i2, w2, 0.0)) / gsum


def _moe_kernel(h_ref, lg_ref, x1_ref, wgu_ref, wd_ref, fg_ref, o_ref, gate_ref, acc_ref, *, final_norm):
    e = pl.program_id(1)

    @pl.when(e == 0)
    def _():
        gate_ref[...] = _route(lg_ref[...])
        acc_ref[...] = jnp.zeros(acc_ref.shape, F32)

    lane = lax.broadcasted_iota(jnp.int32, gate_ref.shape, 1)
    gcol = jnp.sum(jnp.where(lane == e + MOE_GROUPS, gate_ref[...], 0.0), -1, keepdims=True)
    hgu = _dot(h_ref[...], wgu_ref[...])
    act = _silu(hgu[:, 0:MOE_HIDDEN]) * hgu[:, MOE_HIDDEN:2 * MOE_HIDDEN] * gcol
    acc_ref[...] += _dot(act.astype(BF16), wd_ref[...])

    @pl.when(e == pl.num_programs(1) - 1)
    def _():
        x2 = x1_ref[...] + acc_ref[...]
        if final_norm:
            x2 = x2 * lax.rsqrt(jnp.mean(x2 * x2, -1, keepdims=True) + NORM_EPS) * fg_ref[...]
        o_ref[...] = x2


def _moe(h2d, logits, x1, lw, final_g, final_norm):
    t = x1.shape[0]
    tm = min(t, 512)
    row = lambda w: pl.BlockSpec((tm, w), lambda i, e: (i, 0))
    return pl.pallas_call(
        functools.partial(_moe_kernel, final_norm=final_norm),
        grid=(t // tm, MOE_EXPERTS),
        in_specs=[row(D_MODEL), row(ROUTER_PAD), row(D_MODEL),
                  pl.BlockSpec((None, D_MODEL, 2 * MOE_HIDDEN), lambda i, e: (e, 0, 0)),
                  pl.BlockSpec((None, MOE_HIDDEN, D_MODEL), lambda i, e: (e, 0, 0)),
                  pl.BlockSpec((1, D_MODEL), lambda i, e: (0, 0))],
        out_specs=row(D_MODEL),
        out_shape=jax.ShapeDtypeStruct((t, D_MODEL), F32),
        scratch_shapes=[pltpu.VMEM((tm, ROUTER_PAD), F32), pltpu.VMEM((tm, D_MODEL), F32)],
        compiler_params=_params(("parallel", "arbitrary")),
        name="moe",
    )(h2d, logits, x1, lw['moe_w_gu'], lw['moe_w_down'], final_g)


def _prep_layer(params, l):
    g = lambda name: params[name][l]
    row = lambda a: a.reshape(1, -1).astype(F32)
    lora_rows = RWKV_PROJ - 3 * MIX

    def lora_pad(w, start):
        return jnp.zeros((lora_rows, MIX), F32).at[start:start + w.shape[0]].set(w)

    pw = g('pool_w')
    pool_wbd = jnp.zeros((MIX, MIX), F32)
    gw = MIX // len(POOL_WINDOWS)
    for gi in range(len(POOL_WINDOWS)):
        pool_wbd = pool_wbd.at[gi * gw:(gi + 1) * gw, gi * gw:(gi + 1) * gw].set(pw[gi])
    router_w = jnp.concatenate([g('moe_router_group_w'), g('moe_router_expert_w')], axis=1)
    router_b = jnp.concatenate([g('moe_router_group_b'), g('moe_router_expert_b')])
    n_r = router_w.shape[1]
    w_mix, w_gate = _cast_w_in(params['w_in'], l)
    lane_pad = (GATE_SHIFT, LANES - GATE_SHIFT)
    return {
        'norm_mix_g': row(g('norm_mix_g')),
        'w_mix': w_mix,
        'w_gate': w_gate,
        'rwkv_mu': row(g('rwkv_mu')), 'rwkv_w0': row(g('rwkv_w0')), 'rwkv_a0': row(g('rwkv_a0')),
        'rwkv_kk': row(g('rwkv_kk')), 'rwkv_ka': row(g('rwkv_ka')),
        'rwkv_w2p': lora_pad(g('rwkv_w2'), 0),
        'rwkv_a2p': lora_pad(g('rwkv_a2'), RWKV_DECAY_LORA),
        'rwkv_g2p': lora_pad(g('rwkv_g2'), RWKV_DECAY_LORA + RWKV_AAA_LORA),
        'rwkv_rk': row(g('rwkv_rk')), 'rwkv_ln_g': row(g('rwkv_ln_g')), 'rwkv_ln_b': row(g('rwkv_ln_b')),
        'pool_wbd': pool_wbd.astype(BF16), 'pool_scale': row(g('pool_scale')),
        'conv_w': g('conv_w'), 'conv_b': row(g('conv_b')),
        'conv_ln_g': row(g('conv_ln_g')), 'conv_ln_b': row(g('conv_ln_b')),
        'ssm_conv_w': g('ssm_conv_w'), 'ssm_conv_b': row(g('ssm_conv_b')),
        'ssm_dt_bias_pad': jnp.pad(row(g('ssm_dt_bias')), ((0, 0), (0, LANES - SSM_HEADS))),
        'ssm_a_log_e': row(jnp.repeat(g('ssm_a_log'), SSM_HEAD_DIM)),
        'ssm_d_e': row(jnp.repeat(g('ssm_d'), SSM_HEAD_DIM)),
        'ssm_norm_g': row(g('ssm_norm_g')),
        'w_branch': jnp.pad(g('w_branch'), ((0, 0), (0, 0), lane_pad)).astype(BF16),
        'w_out': jnp.pad(g('w_out'), (lane_pad, (0, 0))).astype(BF16),
        'norm_ffn_g': row(g('norm_ffn_g')),
        'router_w': jnp.pad(router_w, ((0, 0), (0, ROUTER_PAD - n_r))),
        'router_b': jnp.pad(row(router_b), ((0, 0), (0, ROUTER_PAD - n_r))),
        'moe_w_gu': jnp.concatenate([g('moe_w_gate'), g('moe_w_up')], axis=-1).astype(BF16),
        'moe_w_down': g('moe_w_down').astype(BF16),
    }


def _layer(x2, st, b, l, pos0, lw, final_g, final_norm):
    shift0, wkv0, pool0, conv0, sconv0, ssm0 = st
    h, p_rwkv, u_pool, s_conv, zx = _inproj(x2, lw['norm_mix_g'], lw['w_mix'])
    seq = lambda a: a.reshape(b, l, a.shape[-1])

    p3 = seq(p_rwkv)
    o_rwkv, wkv1t = _rwkv_scan(_rwkv_prep(p3, shift0, lw), lw, _wkv_to_pairs(wkv0))
    o_pool, pool1 = _pool(seq(u_pool), pool0, pos0, lw)
    o_conv, conv1 = _conv(seq(s_conv), conv0, lw)
    zx3 = seq(zx)
    xbc_raw = zx3[:, :, MIX:MIX + SSM_CONV_DIM]
    if l >= SSM_CONV_HIST:
        sconv1 = xbc_raw[:, l - SSM_CONV_HIST:]
    else:
        sconv1 = jnp.concatenate([sconv0, xbc_raw], axis=1)[:, -SSM_CONV_HIST:]
    lpad = -l % SUBLANES
    zx3p = jnp.pad(zx3, ((0, 0), (0, lpad), (0, 0))) if lpad else zx3
    o_ssm, ssm1t = _ssd(zx3p, sconv0, _ssm_to_rows(ssm0), l, lw)
    o_ssm = o_ssm[:, :l]

    flat = lambda a: a.reshape(b * l, MIX)
    x1, h2, logits = _merge(x2, h, [flat(o_rwkv), flat(o_pool), flat(o_conv), flat(o_ssm)], lw)
    x_out = _moe(h2, logits, x1, lw, final_g, final_norm)
    return x_out, (p3[:, -1], _wkv_from_pairs(wkv1t), pool1, conv1, sconv1, _ssm_from_rows(ssm1t))


def _trunk(x, states, pos0, layers, final_g):
    b, l, _ = x.shape
    x2 = x.reshape(b * l, D_MODEL)
    new = [[] for _ in states]
    for li, lw in enumerate(layers):
        st = tuple(s[li] for s in states)
        x2, st1 = _layer(x2, st, b, l, pos0, lw, final_g, li == len(layers) - 1)
        for lst, s in zip(new, st1):
            lst.append(s)
    return x2.reshape(b, l, D_MODEL), tuple(jnp.stack(lst) for lst in new)


def kernel(x_prompt, x_sample, state_rwkv_shift, state_rwkv_wkv, state_pool, state_conv, state_ssm_conv, state_ssm,
           norm_mix_g, w_in, rwkv_mu, rwkv_w0, rwkv_w2, rwkv_a0, rwkv_a2, rwkv_g2, rwkv_kk, rwkv_ka, rwkv_rk,
           rwkv_ln_g, rwkv_ln_b, pool_w, pool_scale, conv_w, conv_b, conv_ln_g, conv_ln_b, ssm_conv_w, ssm_conv_b,
           ssm_dt_bias, ssm_a_log, ssm_d, ssm_norm_g, w_branch, w_out, norm_ffn_g, moe_router_group_w,
           moe_router_group_b, moe_router_expert_w, moe_router_expert_b, moe_w_gate, moe_w_up, moe_w_down,
           final_norm_g):
    params = {
        'norm_mix_g': norm_mix_g, 'w_in': w_in, 'rwkv_mu': rwkv_mu, 'rwkv_w0': rwkv_w0, 'rwkv_w2': rwkv_w2,
        'rwkv_a0': rwkv_a0, 'rwkv_a2': rwkv_a2, 'rwkv_g2': rwkv_g2, 'rwkv_kk': rwkv_kk, 'rwkv_ka': rwkv_ka,
        'rwkv_rk': rwkv_rk, 'rwkv_ln_g': rwkv_ln_g, 'rwkv_ln_b': rwkv_ln_b, 'pool_w': pool_w,
        'pool_scale': pool_scale, 'conv_w': conv_w, 'conv_b': conv_b, 'conv_ln_g': conv_ln_g,
        'conv_ln_b': conv_ln_b, 'ssm_conv_w': ssm_conv_w, 'ssm_conv_b': ssm_conv_b, 'ssm_dt_bias': ssm_dt_bias,
        'ssm_a_log': ssm_a_log, 'ssm_d': ssm_d, 'ssm_norm_g': ssm_norm_g, 'w_branch': w_branch, 'w_out': w_out,
        'norm_ffn_g': norm_ffn_g, 'moe_router_group_w': moe_router_group_w,
        'moe_router_group_b': moe_router_group_b, 'moe_router_expert_w': moe_router_expert_w,
        'moe_router_expert_b': moe_router_expert_b, 'moe_w_gate': moe_w_gate, 'moe_w_up': moe_w_up,
        'moe_w_down': moe_w_down,
    }
    layers = [_prep_layer(params, l) for l in range(DEPTH)]
    final_g = final_norm_g.reshape(1, D_MODEL).astype(F32)
    bp, dtp = x_prompt.shape[0], x_prompt.dtype
    empty = (jnp.zeros((DEPTH, bp, RWKV_PROJ), dtp),
             jnp.zeros((DEPTH, bp, RWKV_HEADS, RWKV_HEAD, RWKV_HEAD), dtp),
             jnp.zeros((DEPTH, bp, POOL_HIST, MIX), dtp),
             jnp.zeros((DEPTH, bp, CONV_HIST, MIX), dtp),
             jnp.zeros((DEPTH, bp, SSM_CONV_HIST, SSM_CONV_DIM), dtp),
             jnp.zeros((DEPTH, bp, SSM_HEADS, SSM_HEAD_DIM, SSM_STATE), dtp))
    y_prompt, p_states = _trunk(x_prompt, empty, 0, layers, final_g)
    past = (state_rwkv_shift, state_rwkv_wkv, state_pool, state_conv, state_ssm_conv, state_ssm)
    y_sample, s_states = _trunk(x_sample, past, PAST_LEN, layers, final_g)
    return (y_prompt, y_sample) + tuple(p_states) + tuple(s_states)
```

```python
import functools
import math

import jax
import jax.numpy as jnp
from jax import lax
from jax.experimental import pallas as pl
from jax.experimental.pallas import tpu as pltpu

F32 = jnp.float32
BF16 = jnp.bfloat16
HIGHEST = lax.Precision.HIGHEST

D_MODEL = 1024
DEPTH = 2
PAST_LEN = 16384
MIX = D_MODEL // 4
N_BRANCH = 4
RWKV_HEAD = 64
RWKV_HEADS = MIX // RWKV_HEAD
RWKV_DECAY_LORA = 32
RWKV_AAA_LORA = 32
RWKV_GATE_LORA = 64
RWKV_PROJ = 3 * MIX + RWKV_DECAY_LORA + RWKV_AAA_LORA + RWKV_GATE_LORA
RWKV_GN_EPS = 64e-5
POOL_WINDOWS = (2, 4, 8, 16)
POOL_HIST = 15
CONV_KERNEL = 31
CONV_HIST = CONV_KERNEL - 1
CONV_LN_EPS = 1e-5
SSM_HEAD_DIM = 64
SSM_HEADS = MIX // SSM_HEAD_DIM
SSM_GROUPS = 2
SSM_STATE = 128
SSM_CONV_KERNEL = 4
SSM_CONV_HIST = SSM_CONV_KERNEL - 1
SSM_CONV_DIM = MIX + 2 * SSM_GROUPS * SSM_STATE
SSM_PROJ = MIX + SSM_CONV_DIM + SSM_HEADS
OFF_POOL = RWKV_PROJ
OFF_CONV = OFF_POOL + MIX
OFF_SSM = OFF_CONV + 2 * MIX
OFF_GATE = OFF_SSM + SSM_PROJ
MOE_GROUPS = 4
MOE_PER_GROUP = 4
MOE_EXPERTS = MOE_GROUPS * MOE_PER_GROUP
MOE_HIDDEN = 256
NORM_EPS = 1e-6

LANES = 128
SUBLANES = 8
VMEM_LIMIT_BYTES = 48 * 1024 * 1024

SSM_IN_PAD = 9 * LANES
MIX_IN_PAD = OFF_SSM + SSM_IN_PAD
ROUTER_PAD = LANES
IN_PROJ = OFF_GATE + N_BRANCH * D_MODEL
GATE_COL0 = OFF_GATE // LANES * LANES
GATE_SHIFT = OFF_GATE - GATE_COL0
GATE_SLAB = -(-(IN_PROJ - GATE_COL0) // LANES) * LANES
GATE_WIN = D_MODEL + LANES
SSD_CHUNK = 128
RWKV_CHUNK = 16
RWKV_BATCH_BLOCK = 8
MOE_EXPERT_BLOCK = MOE_PER_GROUP
NEG_BIG = -1e30


def _sigmoid(x):
    return 1.0 / (1.0 + jnp.exp(-x))


def _silu(x):
    return x * _sigmoid(x)


def _softplus(x):
    return jnp.maximum(x, 0.0) + jnp.log(1.0 + jnp.exp(-jnp.abs(x)))


def _dot(a, b):
    return jnp.dot(a, b, preferred_element_type=F32)


def _dot_hi(a, b):
    return jnp.dot(a, b, preferred_element_type=F32, precision=HIGHEST)


def _dot_nt(a, b, precision=None):
    return lax.dot_general(a, b, (((1,), (1,)), ((), ())), preferred_element_type=F32, precision=precision)


def _eye(n):
    return (lax.broadcasted_iota(jnp.int32, (n, n), 0) == lax.broadcasted_iota(jnp.int32, (n, n), 1)).astype(F32)


def _head_ones(n, head):
    r = lax.broadcasted_iota(jnp.int32, (n, n), 0) // head
    c = lax.broadcasted_iota(jnp.int32, (n, n), 1) // head
    return (r == c).astype(F32)


def _params(sem):
    return pltpu.CompilerParams(dimension_semantics=sem, vmem_limit_bytes=VMEM_LIMIT_BYTES)


def _const_spec(shape):
    nd = len(shape)
    return pl.BlockSpec(shape, lambda *_: (0,) * nd)


def _cast_w_in_kernel(w_ref, mix_ref, gate_ref):
    rows = w_ref.shape[0]
    lane = lax.broadcasted_iota(jnp.int32, (rows, LANES), 1)
    mix_ref[:, 0:GATE_COL0] = w_ref[:, 0:GATE_COL0].astype(BF16)
    tail = jnp.where(lane < GATE_SHIFT, w_ref[:, GATE_COL0:GATE_COL0 + LANES], 0.0)
    mix_ref[:, GATE_COL0:MIX_IN_PAD] = tail.astype(BF16)
    whole = (IN_PROJ - GATE_COL0) // LANES * LANES
    gate_ref[:, 0:whole] = w_ref[:, GATE_COL0:GATE_COL0 + whole].astype(BF16)
    gate_ref[:, whole:GATE_SLAB] = jnp.zeros((rows, GATE_SLAB - whole), BF16)
    gate_ref[:, whole:IN_PROJ - GATE_COL0] = w_ref[:, GATE_COL0 + whole:IN_PROJ].astype(BF16)


def _cast_w_in(w_in, l):
    tr = 128
    return pl.pallas_call(
        _cast_w_in_kernel,
        grid=(D_MODEL // tr,),
        in_specs=[pl.BlockSpec((None, tr, IN_PROJ), lambda i: (l, i, 0))],
        out_specs=[pl.BlockSpec((tr, MIX_IN_PAD), lambda i: (i, 0)), pl.BlockSpec((tr, GATE_SLAB), lambda i: (i, 0))],
        out_shape=[jax.ShapeDtypeStruct((D_MODEL, MIX_IN_PAD), BF16),
                   jax.ShapeDtypeStruct((D_MODEL, GATE_SLAB), BF16)],
        compiler_params=_params(("parallel",)),
        name="cast_w_in",
    )(w_in)


def _inproj_kernel(x_ref, g_ref, w_ref, h_ref, rw_ref, pool_ref, conv_ref, ssm_ref):
    x = x_ref[...]
    y = x * lax.rsqrt(jnp.mean(x * x, -1, keepdims=True) + NORM_EPS) * g_ref[...]
    hb = y.astype(BF16)
    h_ref[...] = hb
    rw_ref[...] = _dot(hb, w_ref[:, 0:OFF_POOL])
    pool_ref[...] = _dot(hb, w_ref[:, OFF_POOL:OFF_CONV])
    conv_ref[...] = _dot(hb, w_ref[:, OFF_CONV:OFF_SSM])
    ssm_ref[...] = _dot(hb, w_ref[:, OFF_SSM:MIX_IN_PAD])


def _inproj(x2, g, w_mix):
    t = x2.shape[0]
    tm = min(t, 512)
    row = lambda w: pl.BlockSpec((tm, w), lambda i: (i, 0))
    return pl.pallas_call(
        _inproj_kernel,
        grid=(t // tm,),
        in_specs=[row(D_MODEL), _const_spec((1, D_MODEL)), _const_spec((D_MODEL, MIX_IN_PAD))],
        out_specs=[row(D_MODEL), row(RWKV_PROJ), row(MIX), row(2 * MIX), row(SSM_IN_PAD)],
        out_shape=[jax.ShapeDtypeStruct((t, D_MODEL), BF16),
                   jax.ShapeDtypeStruct((t, RWKV_PROJ), F32),
                   jax.ShapeDtypeStruct((t, MIX), F32),
                   jax.ShapeDtypeStruct((t, 2 * MIX), F32),
                   jax.ShapeDtypeStruct((t, SSM_IN_PAD), F32)],
        compiler_params=_params(("parallel",)),
        name="inproj",
    )(x2, g, w_mix)


def _rwkv_prep_math(p, prev, mu, w0, a0, kkw, kaw, w2p, a2p, g2p):
    q = p + (prev - p) * mu
    r = q[:, 0:MIX]
    k = q[:, MIX:2 * MIX]
    v = q[:, 2 * MIX:3 * MIX]
    lora = q[:, 3 * MIX:RWKV_PROJ]
    zw = w0 + _dot_hi(jnp.tanh(lora), w2p)
    decay = jnp.exp(-math.exp(-0.5) * _sigmoid(zw))
    a = _sigmoid(a0 + _dot_hi(lora, a2p))
    g = _dot_hi(_sigmoid(lora), g2p)
    kk = k * kkw
    ss = _dot_hi(kk * kk, _head_ones(MIX, RWKV_HEAD))
    kk = kk * lax.rsqrt(jnp.maximum(ss, 1e-24))
    k2 = k * (1.0 + (a - 1.0) * kaw)
    return r, k2, v, kk, kk * a, decay, g


def _rwkv_prep_seq_kernel(p_ref, s0_ref, mu_ref, w0_ref, a0_ref, kkw_ref, kaw_ref, w2_ref, a2_ref, g2_ref,
                          r_ref, k_ref, v_ref, kk_ref, b_ref, d_ref, g_ref, carry_ref):
    @pl.when(pl.program_id(1) == 0)
    def _():
        carry_ref[...] = s0_ref[...]

    p = p_ref[...]
    rows = p.shape[0]
    first = lax.broadcasted_iota(jnp.int32, p.shape, 0) == 0
    prev = jnp.where(first, carry_ref[...], pltpu.roll(p, 1, 0))
    carry_ref[...] = p[rows - 1:rows, :]
    outs = _rwkv_prep_math(p, prev, mu_ref[...], w0_ref[...], a0_ref[...], kkw_ref[...], kaw_ref[...],
                           w2_ref[...], a2_ref[...], g2_ref[...])
    for ref, val in zip((r_ref, k_ref, v_ref, kk_ref, b_ref, d_ref, g_ref), outs):
        ref[...] = val


def _rwkv_prep_tok_kernel(p_ref, prev_ref, mu_ref, w0_ref, a0_ref, kkw_ref, kaw_ref, w2_ref, a2_ref, g2_ref,
                          r_ref, k_ref, v_ref, kk_ref, b_ref, d_ref, g_ref):
    outs = _rwkv_prep_math(p_ref[...], prev_ref[...], mu_ref[...], w0_ref[...], a0_ref[...], kkw_ref[...],
                           kaw_ref[...], w2_ref[...], a2_ref[...], g2_ref[...])
    for ref, val in zip((r_ref, k_ref, v_ref, kk_ref, b_ref, d_ref, g_ref), outs):
        ref[...] = val


def _rwkv_prep(p3, shift0, lw):
    b, l, _ = p3.shape
    consts = [lw['rwkv_mu'], lw['rwkv_w0'], lw['rwkv_a0'], lw['rwkv_kk'], lw['rwkv_ka'],
              lw['rwkv_w2p'], lw['rwkv_a2p'], lw['rwkv_g2p']]
    const_specs = [_const_spec(c.shape) for c in consts]
    if l == 1:
        outs = pl.pallas_call(
            _rwkv_prep_tok_kernel,
            grid=(1,),
            in_specs=[_const_spec((b, RWKV_PROJ)), _const_spec((b, RWKV_PROJ))] + const_specs,
            out_specs=[_const_spec((b, MIX))] * 7,
            out_shape=[jax.ShapeDtypeStruct((b, MIX), F32)] * 7,
            compiler_params=_params(("arbitrary",)),
            name="rwkv_prep_tok",
        )(p3[:, 0], shift0, *consts)
        return [o[:, None, :] for o in outs]
    lt = min(l, 512)
    seq = lambda w: pl.BlockSpec((None, lt, w), lambda i, j: (i, j, 0))
    return pl.pallas_call(
        _rwkv_prep_seq_kernel,
        grid=(b, l // lt),
        in_specs=[seq(RWKV_PROJ), pl.BlockSpec((None, 1, RWKV_PROJ), lambda i, j: (i, 0, 0))] + const_specs,
        out_specs=[seq(MIX)] * 7,
        out_shape=[jax.ShapeDtypeStruct((b, l, MIX), F32)] * 7,
        scratch_shapes=[pltpu.VMEM((1, RWKV_PROJ), F32)],
        compiler_params=_params(("parallel", "arbitrary")),
        name="rwkv_prep_seq",
    )(p3, shift0[:, None, :], *consts)


def _rwkv_scan_kernel(r_ref, k_ref, v_ref, kk_ref, b_ref, d_ref, g_ref, lng_ref, lnb_ref, rk_ref, s0_ref,
                      o_ref, s1_ref, st_ref, col_ref, row_ref, y_ref, *, nb, lc):
    c = pl.program_id(2)
    n = RWKV_HEAD

    @pl.when(c == 0)
    def _():
        for bb in range(nb):
            st_ref[bb] = _dot_nt(_eye(n), s0_ref[bb], HIGHEST)

    ones = _head_ones(LANES, n).astype(BF16)
    ones2 = jnp.concatenate([ones, ones], axis=0)
    lane = lax.broadcasted_iota(jnp.int32, (n, LANES), 1)
    sub = lax.broadcasted_iota(jnp.int32, (n, LANES), 0)
    diag = (lane % n == sub).astype(BF16)
    low = lax.broadcasted_iota(jnp.int32, (lc, LANES), 1) < n

    def head_sum(x):
        s_lo = jnp.sum(jnp.where(low, x, 0.0), -1, keepdims=True)
        s_hi = jnp.sum(jnp.where(low, 0.0, x), -1, keepdims=True)
        return jnp.where(low, s_lo, s_hi)

    def spread2(x):
        hi = x.astype(BF16)
        lo = (x - hi.astype(F32)).astype(BF16)
        parts = [(p[:, None, :] * diag[None, :, :]).reshape(lc * n, LANES) for p in (hi, lo)]
        return jnp.concatenate(parts, axis=-1)

    for bb in range(nb):
        r = r_ref[bb]
        k = k_ref[bb]
        kk = kk_ref[bb]
        bv = b_ref[bb]
        d = d_ref[bb]
        for idx, x in enumerate((kk, d, bv, k, d * r)):
            col_ref[idx, bb] = _dot(spread2(x), ones2)
        row_ref[0, bb] = head_sum(bv * r)
        row_ref[1, bb] = head_sum(k * r)

    def step(t, carry):
        base = pl.multiple_of(t * n, n)
        for bb in range(nb):
            st = st_ref[bb]
            ckk = col_ref[0, bb, pl.ds(base, n), :]
            cd = col_ref[1, bb, pl.ds(base, n), :]
            cb = col_ref[2, bb, pl.ds(base, n), :]
            ck = col_ref[3, bb, pl.ds(base, n), :]
            cdr = col_ref[4, bb, pl.ds(base, n), :]
            vrow = v_ref[bb, pl.ds(t, 1), :]
            br = row_ref[0, bb, pl.ds(t, 1), :]
            kr = row_ref[1, bb, pl.ds(t, 1), :]
            sa = -jnp.sum(st * ckk, axis=0, keepdims=True)
            y = jnp.sum(st * cdr, axis=0, keepdims=True) + sa * br + vrow * kr
            st_ref[bb] = st * cd + sa * cb + vrow * ck
            y_ref[bb, pl.ds(t, 1), :] = y
        return carry

    lax.fori_loop(0, lc, step, 0)

    lng = lng_ref[...]
    lnb = lnb_ref[...]
    rk = rk_ref[...]
    inv_n = 1.0 / n
    for bb in range(nb):
        y = y_ref[bb]
        mu = head_sum(y) * inv_n
        yc = y - mu
        var = head_sum(yc * yc) * inv_n
        yn = yc * lax.rsqrt(var + RWKV_GN_EPS) * lng + lnb
        r = r_ref[bb]
        v = v_ref[bb]
        rkv = head_sum(r * k_ref[bb] * rk)
        o_ref[bb] = (yn + rkv * v) * g_ref[bb]

    @pl.when(c == pl.num_programs(2) - 1)
    def _():
        for bb in range(nb):
            s1_ref[bb] = _dot_nt(_eye(LANES), st_ref[bb], HIGHEST)


def _rwkv_scan(prep, lw, s0t):
    r, k, v, kk, bv, d, g = prep
    b, l, _ = r.shape
    nb = RWKV_BATCH_BLOCK
    lc = min(l, RWKV_CHUNK)
    n = RWKV_HEAD
    seq = pl.BlockSpec((nb, lc, LANES), lambda i, p, c: (i, c, p))
    vec = pl.BlockSpec((1, LANES), lambda i, p, c: (0, p))
    st = pl.BlockSpec((nb, None, LANES, n), lambda i, p, c: (i, p, 0, 0))
    kern = functools.partial(_rwkv_scan_kernel, nb=nb, lc=lc)
    return pl.pallas_call(
        kern,
        grid=(b // nb, 2, l // lc),
        in_specs=[seq] * 7 + [vec] * 3 + [st],
        out_specs=[seq, st],
        out_shape=[jax.ShapeDtypeStruct((b, l, MIX), F32), jax.ShapeDtypeStruct((b, 2, LANES, n), F32)],
        scratch_shapes=[pltpu.VMEM((nb, n, LANES), F32),
                        pltpu.VMEM((5, nb, lc * n, LANES), F32),
                        pltpu.VMEM((2, nb, lc, LANES), F32),
                        pltpu.VMEM((nb, lc, LANES), F32)],
        compiler_params=_params(("parallel", "parallel", "arbitrary")),
        name="rwkv_scan",
    )(r, k, v, kk, bv, d, g, lw['rwkv_ln_g'], lw['rwkv_ln_b'], lw['rwkv_rk'], s0t)


def _wkv_to_pairs(s):
    return s.reshape(s.shape[0], 2, 2 * RWKV_HEAD, RWKV_HEAD)


def _wkv_from_pairs(s):
    return s.reshape(s.shape[0], RWKV_HEADS, RWKV_HEAD, RWKV_HEAD)


def _pool_math(tap, pos, pw, scale):
    rows = pos.shape[0]
    lane = lax.broadcasted_iota(jnp.int32, (rows, LANES), 1)
    low = lane < (LANES // 2)
    cnt = lambda w: jnp.minimum(pos + 1, w).astype(F32)
    a0 = tap(0, 0)
    s2 = a0 + tap(1, 0)
    s4 = s2 + tap(2, 0) + tap(3, 0)
    b0 = tap(0, 1)
    s8 = b0
    for k in range(1, 8):
        s8 = s8 + tap(k, 1)
    s16 = s8
    for k in range(8, 16):
        s16 = s16 + tap(k, 1)
    diff_a = jnp.where(low, s2 / cnt(2), s4 / cnt(4)) - a0
    diff_b = jnp.where(low, s8 / cnt(8), s16 / cnt(16)) - b0
    diff = jnp.concatenate([diff_a, diff_b], axis=-1).astype(BF16)
    return _dot(diff, pw) * scale


def _pool_seq_kernel(u_ref, h_ref, pw_ref, sc_ref, o_ref, h1_ref, full_ref, *, l, tt):
    pad = POOL_HIST + 1
    full_ref[0:1, :] = jnp.zeros((1, MIX), F32)
    full_ref[1:pad, :] = h_ref[...]
    full_ref[pad:pad + l, :] = u_ref[...]
    pw = pw_ref[...]
    sc = sc_ref[...]
    for i in range(l // tt):
        t0 = i * tt
        tap = lambda k, half: full_ref[pad + t0 - k:pad + t0 - k + tt, half * LANES:(half + 1) * LANES]
        pos = t0 + lax.broadcasted_iota(jnp.int32, (tt, 1), 0)
        o_ref[t0:t0 + tt, :] = _pool_math(tap, pos, pw, sc)
    h1_ref[...] = full_ref[l + 1:l + pad, :]


def _pool_tok_kernel(u_ref, ht_ref, pw_ref, sc_ref, o_ref, *, pos0):
    def tap(k, half):
        sl = slice(half * LANES, (half + 1) * LANES)
        return u_ref[:, sl] if k == 0 else ht_ref[POOL_HIST - k, :, sl]
    rows = u_ref.shape[0]
    pos = jnp.full((rows, 1), pos0, jnp.int32)
    o_ref[...] = _pool_math(tap, pos, pw_ref[...], sc_ref[...])


def _pool(u3, hist, pos0, lw):
    b, l, _ = u3.shape
    consts = [lw['pool_wbd'], lw['pool_scale']]
    const_specs = [_const_spec(c.shape) for c in consts]
    if l == 1:
        ht = jnp.swapaxes(hist, 0, 1)
        out = pl.pallas_call(
            functools.partial(_pool_tok_kernel, pos0=pos0),
            grid=(1,),
            in_specs=[_const_spec((b, MIX)), _const_spec((POOL_HIST, b, MIX))] + const_specs,
            out_specs=_const_spec((b, MIX)),
            out_shape=jax.ShapeDtypeStruct((b, MIX), F32),
            compiler_params=_params(("arbitrary",)),
            name="pool_tok",
        )(u3[:, 0], ht, *consts)
        return out[:, None, :], jnp.concatenate([hist[:, 1:], u3], axis=1)
    assert pos0 == 0 and l >= POOL_HIST
    tt = min(l, 256)
    seq = pl.BlockSpec((None, l, MIX), lambda i: (i, 0, 0))
    hs = pl.BlockSpec((None, POOL_HIST, MIX), lambda i: (i, 0, 0))
    return pl.pallas_call(
        functools.partial(_pool_seq_kernel, l=l, tt=tt),
        grid=(b,),
        in_specs=[seq, hs] + const_specs,
        out_specs=[seq, hs],
        out_shape=[jax.ShapeDtypeStruct((b, l, MIX), F32), jax.ShapeDtypeStruct((b, POOL_HIST, MIX), F32)],
        scratch_shapes=[pltpu.VMEM((l + POOL_HIST + 1, MIX), F32)],
        compiler_params=_params(("parallel",)),
        name="pool_seq",
    )(u3, hist, *consts)


def _conv_post(y, cb, g, beta):
    y = y + cb
    mu = jnp.mean(y, -1, keepdims=True)
    yc = y - mu
    var = jnp.mean(yc * yc, -1, keepdims=True)
    return _silu(yc * lax.rsqrt(var + CONV_LN_EPS) * g + beta)


def _glu(s):
    return s[:, 0:MIX] * _sigmoid(s[:, MIX:2 * MIX])


def _conv_seq_kernel(s_ref, h_ref, w_ref, cb_ref, g_ref, beta_ref, o_ref, h1_ref, full_ref, *, l, tt):
    pad = CONV_HIST + 2
    full_ref[0:2, :] = jnp.zeros((2, MIX), F32)
    full_ref[2:pad, :] = h_ref[...]
    full_ref[pad:pad + l, :] = _glu(s_ref[...])
    cb = cb_ref[...]
    g = g_ref[...]
    beta = beta_ref[...]

    for i in range(l // tt):
        t0 = i * tt
        acc = jnp.zeros((tt, MIX), F32)
        for k in range(CONV_KERNEL):
            acc = acc + w_ref[k:k + 1, :] * full_ref[t0 + k + 2:t0 + k + 2 + tt, :]
        o_ref[t0:t0 + tt, :] = _conv_post(acc, cb, g, beta)
    h1_ref[...] = full_ref[l + 2:l + pad, :]


def _conv_tok_kernel(s_ref, ht_ref, w_ref, cb_ref, g_ref, beta_ref, o_ref, c_ref):
    c = _glu(s_ref[...])
    c_ref[...] = c
    acc = w_ref[CONV_HIST:CONV_KERNEL, :] * c
    for k in range(CONV_HIST):
        acc = acc + w_ref[k:k + 1, :] * ht_ref[k]
    o_ref[...] = _conv_post(acc, cb_ref[...], g_ref[...], beta_ref[...])


def _conv(s3, hist, lw):
    b, l, _ = s3.shape
    consts = [lw['conv_w'], lw['conv_b'], lw['conv_ln_g'], lw['conv_ln_b']]
    const_specs = [_const_spec(c.shape) for c in consts]
    if l == 1:
        ht = jnp.swapaxes(hist, 0, 1)
        out, c = pl.pallas_call(
            _conv_tok_kernel,
            grid=(1,),
            in_specs=[_const_spec((b, 2 * MIX)), _const_spec((CONV_HIST, b, MIX))] + const_specs,
            out_specs=[_const_spec((b, MIX))] * 2,
            out_shape=[jax.ShapeDtypeStruct((b, MIX), F32)] * 2,
            compiler_params=_params(("arbitrary",)),
            name="conv_tok",
        )(s3[:, 0], ht, *consts)
        return out[:, None, :], jnp.concatenate([hist[:, 1:], c[:, None, :]], axis=1)
    assert l >= CONV_HIST
    tt = min(l, 64)
    hs = pl.BlockSpec((None, CONV_HIST, MIX), lambda i: (i, 0, 0))
    return pl.pallas_call(
        functools.partial(_conv_seq_kernel, l=l, tt=tt),
        grid=(b,),
        in_specs=[pl.BlockSpec((None, l, 2 * MIX), lambda i: (i, 0, 0)), hs] + const_specs,
        out_specs=[pl.BlockSpec((None, l, MIX), lambda i: (i, 0, 0)), hs],
        out_shape=[jax.ShapeDtypeStruct((b, l, MIX), F32), jax.ShapeDtypeStruct((b, CONV_HIST, MIX), F32)],
        scratch_shapes=[pltpu.VMEM((l + CONV_HIST + 2, MIX), F32)],
        compiler_params=_params(("parallel",)),
        name="conv_seq",
    )(s3, hist, *consts)


def _ssd_kernel(zx_ref, ch_ref, h0_ref, cw_ref, cb_ref, dtb_ref, alog_ref, dd_ref, ng_ref,
                o_ref, h1_ref, xf_ref, ht_ref, *, lb, lvalid):
    q = SSD_CHUNK
    c = pl.program_id(1)
    halo = SUBLANES
    gw = SSM_GROUPS * SSM_STATE

    @pl.when(c == 0)
    def _():
        ht_ref[...] = h0_ref[...]
        xf_ref[...] = jnp.zeros(xf_ref.shape, F32)
        xf_ref[halo - SSM_CONV_HIST:halo, :] = ch_ref[...]

    xf_ref[halo:halo + lb, :] = zx_ref[:, MIX:MIX + SSM_CONV_DIM]
    acc = jnp.zeros((q, SSM_CONV_DIM), F32)
    for k in range(SSM_CONV_KERNEL):
        acc = acc + cw_ref[k:k + 1, :] * xf_ref[halo - SSM_CONV_HIST + k:halo - SSM_CONV_HIST + k + q, :]
    xbc = _silu(acc + cb_ref[...])
    xf_ref[halo - SSM_CONV_HIST:halo, :] = xf_ref[halo + q - SSM_CONV_HIST:halo + q, :]

    xs = xbc[:, 0:MIX]
    bm = xbc[:, MIX:MIX + gw]
    cm = xbc[:, MIX + gw:MIX + 2 * gw]

    hsel = (lax.broadcasted_iota(jnp.int32, (LANES, MIX), 0)
            == lax.broadcasted_iota(jnp.int32, (LANES, MIX), 1) // SSM_HEAD_DIM).astype(F32)
    dt_raw = zx_ref[:, MIX + SSM_CONV_DIM:SSM_IN_PAD]
    if lb < q:
        dt_raw = jnp.concatenate([dt_raw, jnp.zeros((q - lb, LANES), F32)], axis=0)
    trow = lax.broadcasted_iota(jnp.int32, (q, 1), 0) + c * q
    dt = jnp.where(trow < lvalid, _softplus(dt_raw + dtb_ref[...]), 0.0)
    dte = _dot_hi(dt, hsel)
    dae = dte * (-jnp.exp(alog_ref[...]))
    ti = lax.broadcasted_iota(jnp.int32, (q, q), 0)
    tj = lax.broadcasted_iota(jnp.int32, (q, q), 1)
    causal = ti >= tj
    acse = _dot_hi(causal.astype(F32), dae)
    xdt = xs * dte
    last = acse[q - 1:q, :]
    w_end = xdt * jnp.exp(last - acse)
    eacs = jnp.exp(acse)

    lane_g = lax.broadcasted_iota(jnp.int32, (q, LANES), 1)
    heads_per_group = SSM_HEADS // SSM_GROUPS
    ys = []
    for g in range(SSM_GROUPS):
        lo = g * LANES
        bg = bm[:, g * SSM_STATE:(g + 1) * SSM_STATE].astype(BF16)
        cg = cm[:, g * SSM_STATE:(g + 1) * SSM_STATE].astype(BF16)
        gmat = _dot_nt(cg, bg)
        hg = ht_ref[lo:lo + LANES, :]
        yg = _dot_nt(cg, hg.astype(BF16)) * eacs[:, lo:lo + LANES]
        xg = xdt[:, lo:lo + LANES]
        for hh in range(heads_per_group):
            col = acse[:, lo + hh * SSM_HEAD_DIM:lo + hh * SSM_HEAD_DIM + 1]
            row = jnp.transpose(jnp.broadcast_to(col, (q, q)))
            seg = jnp.where(causal, col - row, NEG_BIG)
            scores = (gmat * jnp.exp(seg)).astype(BF16)
            xm = jnp.where(lane_g // SSM_HEAD_DIM == hh, xg, 0.0).astype(BF16)
            yg = yg + _dot(scores, xm)
        ys.append(yg)
        st = _dot(jnp.transpose(w_end[:, lo:lo + LANES]).astype(BF16), bg)
        for hh in range(heads_per_group):
            r0 = hh * SSM_HEAD_DIM
            decay = jnp.exp(last[:, lo + r0:lo + r0 + 1])
            ht_ref[lo + r0:lo + r0 + SSM_HEAD_DIM, :] = hg[r0:r0 + SSM_HEAD_DIM] * decay + st[r0:r0 + SSM_HEAD_DIM]
    y = jnp.concatenate(ys, axis=-1)

    y = y + dd_ref[...] * xs
    z = zx_ref[:, 0:MIX]
    if lb < q:
        z = jnp.concatenate([z, jnp.zeros((q - lb, MIX), F32)], axis=0)
    y = y * _silu(z)
    y = y * lax.rsqrt(jnp.mean(y * y, -1, keepdims=True) + NORM_EPS) * ng_ref[...]
    o_ref[...] = y[0:lb, :]

    @pl.when(c == pl.num_programs(1) - 1)
    def _():
        h1_ref[...] = ht_ref[...]


def _ssd(zx3, conv_hist, h0t, lvalid, lw):
    b, lp, _ = zx3.shape
    q = SSD_CHUNK
    lb = min(lp, q)
    consts = [lw['ssm_conv_w'], lw['ssm_conv_b'], lw['ssm_dt_bias_pad'], lw['ssm_a_log_e'], lw['ssm_d_e'],
              lw['ssm_norm_g']]
    const_specs = [pl.BlockSpec(cst.shape, lambda i, c: (0, 0)) for cst in consts]
    hs = pl.BlockSpec((None, MIX, SSM_STATE), lambda i, c: (i, 0, 0))
    return pl.pallas_call(
        functools.partial(_ssd_kernel, lb=lb, lvalid=lvalid),
        grid=(b, lp // lb),
        in_specs=[pl.BlockSpec((None, lb, SSM_IN_PAD), lambda i, c: (i, c, 0)),
                  pl.BlockSpec((None, SSM_CONV_HIST, SSM_CONV_DIM), lambda i, c: (i, 0, 0)), hs] + const_specs,
        out_specs=[pl.BlockSpec((None, lb, MIX), lambda i, c: (i, c, 0)), hs],
        out_shape=[jax.ShapeDtypeStruct((b, lp, MIX), F32), jax.ShapeDtypeStruct((b, MIX, SSM_STATE), F32)],
        scratch_shapes=[pltpu.VMEM((SUBLANES + q, SSM_CONV_DIM), F32), pltpu.VMEM((MIX, SSM_STATE), F32)],
        compiler_params=_params(("parallel", "arbitrary")),
        name="ssd",
    )(zx3, conv_hist, h0t, *consts)


def _ssm_to_rows(s):
    return s.reshape(s.shape[0], MIX, SSM_STATE)


def _ssm_from_rows(s):
    return s.reshape(s.shape[0], SSM_HEADS, SSM_HEAD_DIM, SSM_STATE)


def _merge_kernel(x_ref, h_ref, b0_ref, b1_ref, b2_ref, b3_ref, wg_ref, wb_ref, wo_ref, ng_ref, wr_ref, br_ref,
                  x1_ref, h2_ref, lg_ref):
    h = h_ref[...]
    merged = None
    for kbr, b_ref in enumerate((b0_ref, b1_ref, b2_ref, b3_ref)):
        gate = _sigmoid(_dot(h, wg_ref[:, kbr * D_MODEL:kbr * D_MODEL + GATE_WIN]))
        up = _dot(b_ref[...].astype(BF16), wb_ref[kbr])
        merged = up * gate if merged is None else merged + up * gate
    x1 = x_ref[...] + _dot(merged.astype(BF16), wo_ref[...])
    x1_ref[...] = x1
    hn = x1 * lax.rsqrt(jnp.mean(x1 * x1, -1, keepdims=True) + NORM_EPS) * ng_ref[...]
    h2_ref[...] = hn.astype(BF16)
    lg_ref[...] = _dot_hi(hn, wr_ref[...]) + br_ref[...]


def _merge(x2, h2d, branches, lw):
    t = x2.shape[0]
    tm = min(t, 256)
    row = lambda w: pl.BlockSpec((tm, w), lambda i: (i, 0))
    consts = [lw['w_gate'], lw['w_branch'], lw['w_out'], lw['norm_ffn_g'], lw['router_w'], lw['router_b']]
    return pl.pallas_call(
        _merge_kernel,
        grid=(t // tm,),
        in_specs=[row(D_MODEL), row(D_MODEL)] + [row(MIX)] * 4 + [_const_spec(c.shape) for c in consts],
        out_specs=[row(D_MODEL), row(D_MODEL), row(ROUTER_PAD)],
        out_shape=[jax.ShapeDtypeStruct((t, D_MODEL), F32), jax.ShapeDtypeStruct((t, D_MODEL), BF16),
                   jax.ShapeDtypeStruct((t, ROUTER_PAD), F32)],
        compiler_params=_params(("parallel",)),
        name="merge",
    )(x2, h2d, *branches, *consts)


def _route(logits):
    lane = lax.broadcasted_iota(jnp.int32, logits.shape, 1).astype(F32)
    far = float(4 * LANES)
    is_group = lane < MOE_GROUPS
    gl = jnp.where(is_group, logits, NEG_BIG)
    gmax = jnp.max(gl, -1, keepdims=True)
    gidx = jnp.min(jnp.where(is_group & (gl == gmax), lane, far), -1, keepdims=True)
    gsum = jnp.sum(jnp.where(is_group, jnp.exp(gl - gmax), 0.0), -1, keepdims=True)
    first = MOE_GROUPS + MOE_PER_GROUP * gidx
    in_group = (lane >= first) & (lane < first + MOE_PER_GROUP)
    el = jnp.where(in_group, logits, NEG_BIG)
    m1 = jnp.max(el, -1, keepdims=True)
    i1 = jnp.min(jnp.where(in_group & (el == m1), lane, far), -1, keepdims=True)
    rest = in_group & (lane != i1)
    el2 = jnp.where(rest, logits, NEG_BIG)
    m2 = jnp.max(el2, -1, keepdims=True)
    i2 = jnp.min(jnp.where(rest & (el2 == m2), lane, far), -1, keepdims=True)
    e2 = jnp.exp(m2 - m1)
    w1 = 1.0 / (1.0 + e2)
    w2 = e2 / (1.0 + e2)
    return (jnp.where(lane == i1, w1, 0.0) + jnp.where(lane == i2, w2, 0.0)) / gsum


def _moe_kernel(h_ref, lg_ref, x1_ref, wgu_ref, wd_ref, fg_ref, o_ref, gate_ref, acc_ref, *, final_norm):
    s = pl.program_id(1)
    eb = wgu_ref.shape[0]

    @pl.when(s == 0)
    def _():
        gate_ref[...] = _route(lg_ref[...])
        acc_ref[...] = jnp.zeros(acc_ref.shape, F32)

    lane = lax.broadcasted_iota(jnp.int32, gate_ref.shape, 1)
    gate = gate_ref[...]
    h = h_ref[...]
    acts = []
    for j in range(eb):
        gcol = jnp.sum(jnp.where(lane == s * eb + j + MOE_GROUPS, gate, 0.0), -1, keepdims=True)
        hgu = _dot(h, wgu_ref[j])
        acts.append((_silu(hgu[:, 0:MOE_HIDDEN]) * hgu[:, MOE_HIDDEN:2 * MOE_HIDDEN] * gcol).astype(BF16))
    act = jnp.concatenate(acts, axis=-1)
    acc_ref[...] += _dot(act, wd_ref[...].reshape(eb * MOE_HIDDEN, D_MODEL))

    @pl.when(s == pl.num_programs(1) - 1)
    def _():
        x2 = x1_ref[...] + acc_ref[...]
        if final_norm:
            x2 = x2 * lax.rsqrt(jnp.mean(x2 * x2, -1, keepdims=True) + NORM_EPS) * fg_ref[...]
        o_ref[...] = x2


def _moe(h2d, logits, x1, lw, final_g, final_norm):
    t = x1.shape[0]
    tm = min(t, 512)
    row = lambda w: pl.BlockSpec((tm, w), lambda i, e: (i, 0))
    return pl.pallas_call(
        functools.partial(_moe_kernel, final_norm=final_norm),
        grid=(t // tm, MOE_EXPERTS // MOE_EXPERT_BLOCK),
        in_specs=[row(D_MODEL), row(ROUTER_PAD), row(D_MODEL),
                  pl.BlockSpec((MOE_EXPERT_BLOCK, D_MODEL, 2 * MOE_HIDDEN), lambda i, e: (e, 0, 0)),
                  pl.BlockSpec((MOE_EXPERT_BLOCK, MOE_HIDDEN, D_MODEL), lambda i, e: (e, 0, 0)),
                  pl.BlockSpec((1, D_MODEL), lambda i, e: (0, 0))],
        out_specs=row(D_MODEL),
        out_shape=jax.ShapeDtypeStruct((t, D_MODEL), F32),
        scratch_shapes=[pltpu.VMEM((tm, ROUTER_PAD), F32), pltpu.VMEM((tm, D_MODEL), F32)],
        compiler_params=_params(("parallel", "arbitrary")),
        name="moe",
    )(h2d, logits, x1, lw['moe_w_gu'], lw['moe_w_down'], final_g)


def _prep_layer(params, l):
    g = lambda name: params[name][l]
    row = lambda a: a.reshape(1, -1).astype(F32)
    lora_rows = RWKV_PROJ - 3 * MIX

    def lora_pad(w, start):
        return jnp.zeros((lora_rows, MIX), F32).at[start:start + w.shape[0]].set(w)

    pw = g('pool_w')
    pool_wbd = jnp.zeros((MIX, MIX), F32)
    gw = MIX // len(POOL_WINDOWS)
    for gi in range(len(POOL_WINDOWS)):
        pool_wbd = pool_wbd.at[gi * gw:(gi + 1) * gw, gi * gw:(gi + 1) * gw].set(pw[gi])
    router_w = jnp.concatenate([g('moe_router_group_w'), g('moe_router_expert_w')], axis=1)
    router_b = jnp.concatenate([g('moe_router_group_b'), g('moe_router_expert_b')])
    n_r = router_w.shape[1]
    w_mix, w_gate = _cast_w_in(params['w_in'], l)
    lane_pad = (GATE_SHIFT, LANES - GATE_SHIFT)
    return {
        'norm_mix_g': row(g('norm_mix_g')),
        'w_mix': w_mix,
        'w_gate': w_gate,
        'rwkv_mu': row(g('rwkv_mu')), 'rwkv_w0': row(g('rwkv_w0')), 'rwkv_a0': row(g('rwkv_a0')),
        'rwkv_kk': row(g('rwkv_kk')), 'rwkv_ka': row(g('rwkv_ka')),
        'rwkv_w2p': lora_pad(g('rwkv_w2'), 0),
        'rwkv_a2p': lora_pad(g('rwkv_a2'), RWKV_DECAY_LORA),
        'rwkv_g2p': lora_pad(g('rwkv_g2'), RWKV_DECAY_LORA + RWKV_AAA_LORA),
        'rwkv_rk': row(g('rwkv_rk')), 'rwkv_ln_g': row(g('rwkv_ln_g')), 'rwkv_ln_b': row(g('rwkv_ln_b')),
        'pool_wbd': pool_wbd.astype(BF16), 'pool_scale': row(g('pool_scale')),
        'conv_w': g('conv_w'), 'conv_b': row(g('conv_b')),
        'conv_ln_g': row(g('conv_ln_g')), 'conv_ln_b': row(g('conv_ln_b')),
        'ssm_conv_w': g('ssm_conv_w'), 'ssm_conv_b': row(g('ssm_conv_b')),
        'ssm_dt_bias_pad': jnp.pad(row(g('ssm_dt_bias')), ((0, 0), (0, LANES - SSM_HEADS))),
        'ssm_a_log_e': row(jnp.repeat(g('ssm_a_log'), SSM_HEAD_DIM)),
        'ssm_d_e': row(jnp.repeat(g('ssm_d'), SSM_HEAD_DIM)),
        'ssm_norm_g': row(g('ssm_norm_g')),
        'w_branch': jnp.pad(g('w_branch'), ((0, 0), (0, 0), lane_pad)).astype(BF16),
        'w_out': jnp.pad(g('w_out'), (lane_pad, (0, 0))).astype(BF16),
        'norm_ffn_g': row(g('norm_ffn_g')),
        'router_w': jnp.pad(router_w, ((0, 0), (0, ROUTER_PAD - n_r))),
        'router_b': jnp.pad(row(router_b), ((0, 0), (0, ROUTER_PAD - n_r))),
        'moe_w_gu': jnp.concatenate([g('moe_w_gate'), g('moe_w_up')], axis=-1).astype(BF16),
        'moe_w_down': g('moe_w_down').astype(BF16),
    }


def _layer(x2, st, b, l, pos0, lw, final_g, final_norm):
    shift0, wkv0, pool0, conv0, sconv0, ssm0 = st
    h, p_rwkv, u_pool, s_conv, zx = _inproj(x2, lw['norm_mix_g'], lw['w_mix'])
    seq = lambda a: a.reshape(b, l, a.shape[-1])

    p3 = seq(p_rwkv)
    o_rwkv, wkv1t = _rwkv_scan(_rwkv_prep(p3, shift0, lw), lw, _wkv_to_pairs(wkv0))
    o_pool, pool1 = _pool(seq(u_pool), pool0, pos0, lw)
    o_conv, conv1 = _conv(seq(s_conv), conv0, lw)
    zx3 = seq(zx)
    xbc_raw = zx3[:, :, MIX:MIX + SSM_CONV_DIM]
    if l >= SSM_CONV_HIST:
        sconv1 = xbc_raw[:, l - SSM_CONV_HIST:]
    else:
        sconv1 = jnp.concatenate([sconv0, xbc_raw], axis=1)[:, -SSM_CONV_HIST:]
    lpad = -l % SUBLANES
    zx3p = jnp.pad(zx3, ((0, 0), (0, lpad), (0, 0))) if lpad else zx3
    o_ssm, ssm1t = _ssd(zx3p, sconv0, _ssm_to_rows(ssm0), l, lw)
    o_ssm = o_ssm[:, :l]

    flat = lambda a: a.reshape(b * l, MIX)
    x1, h2, logits = _merge(x2, h, [flat(o_rwkv), flat(o_pool), flat(o_conv), flat(o_ssm)], lw)
    x_out = _moe(h2, logits, x1, lw, final_g, final_norm)
    return x_out, (p3[:, -1], _wkv_from_pairs(wkv1t), pool1, conv1, sconv1, _ssm_from_rows(ssm1t))


def _trunk(x, states, pos0, layers, final_g):
    b, l, _ = x.shape
    x2 = x.reshape(b * l, D_MODEL)
    new = [[] for _ in states]
    for li, lw in enumerate(layers):
        st = tuple(s[li] for s in states)
        x2, st1 = _layer(x2, st, b, l, pos0, lw, final_g, li == len(layers) - 1)
        for lst, s in zip(new, st1):
            lst.append(s)
    return x2.reshape(b, l, D_MODEL), tuple(jnp.stack(lst) for lst in new)


def kernel(x_prompt, x_sample, state_rwkv_shift, state_rwkv_wkv, state_pool, state_conv, state_ssm_conv, state_ssm,
           norm_mix_g, w_in, rwkv_mu, rwkv_w0, rwkv_w2, rwkv_a0, rwkv_a2, rwkv_g2, rwkv_kk, rwkv_ka, rwkv_rk,
           rwkv_ln_g, rwkv_ln_b, pool_w, pool_scale, conv_w, conv_b, conv_ln_g, conv_ln_b, ssm_conv_w, ssm_conv_b,
           ssm_dt_bias, ssm_a_log, ssm_d, ssm_norm_g, w_branch, w_out, norm_ffn_g, moe_router_group_w,
           moe_router_group_b, moe_router_expert_w, moe_router_expert_b, moe_w_gate, moe_w_up, moe_w_down,
           final_norm_g):
    params = {
        'norm_mix_g': norm_mix_g, 'w_in': w_in, 'rwkv_mu': rwkv_mu, 'rwkv_w0': rwkv_w0, 'rwkv_w2': rwkv_w2,
        'rwkv_a0': rwkv_a0, 'rwkv_a2': rwkv_a2, 'rwkv_g2': rwkv_g2, 'rwkv_kk': rwkv_kk, 'rwkv_ka': rwkv_ka,
        'rwkv_rk': rwkv_rk, 'rwkv_ln_g': rwkv_ln_g, 'rwkv_ln_b': rwkv_ln_b, 'pool_w': pool_w,
        'pool_scale': pool_scale, 'conv_w': conv_w, 'conv_b': conv_b, 'conv_ln_g': conv_ln_g,
        'conv_ln_b': conv_ln_b, 'ssm_conv_w': ssm_conv_w, 'ssm_conv_b': ssm_conv_b, 'ssm_dt_bias': ssm_dt_bias,
        'ssm_a_log': ssm_a_log, 'ssm_d': ssm_d, 'ssm_norm_g': ssm_norm_g, 'w_branch': w_branch, 'w_out': w_out,
        'norm_ffn_g': norm_ffn_g, 'moe_router_group_w': moe_router_group_w,
        'moe_router_group_b': moe_router_group_b, 'moe_router_expert_w': moe_router_expert_w,
        'moe_router_expert_b': moe_router_expert_b, 'moe_w_gate': moe_w_gate, 'moe_w_up': moe_w_up,
        'moe_w_down': moe_w_down,
    }
    layers = [_prep_layer(params, l) for l in range(DEPTH)]
    final_g = final_norm_g.reshape(1, D_MODEL).astype(F32)
    bp, dtp = x_prompt.shape[0], x_prompt.dtype
    empty = (jnp.zeros((DEPTH, bp, RWKV_PROJ), dtp),
             jnp.zeros((DEPTH, bp, RWKV_HEADS, RWKV_HEAD, RWKV_HEAD), dtp),
             jnp.zeros((DEPTH, bp, POOL_HIST, MIX), dtp),
             jnp.zeros((DEPTH, bp, CONV_HIST, MIX), dtp),
             jnp.zeros((DEPTH, bp, SSM_CONV_HIST, SSM_CONV_DIM), dtp),
             jnp.zeros((DEPTH, bp, SSM_HEADS, SSM_HEAD_DIM, SSM_STATE), dtp))
    y_prompt, p_states = _trunk(x_prompt, empty, 0, layers, final_g)
    past = (state_rwkv_shift, state_rwkv_wkv, state_pool, state_conv, state_ssm_conv, state_ssm)
    y_sample, s_states = _trunk(x_sample, past, PAST_LEN, layers, final_g)
    return (y_prompt, y_sample) + tuple(p_states) + tuple(s_states)
```

```python
import functools
import math

import jax
import jax.numpy as jnp
from jax import lax
from jax.experimental import pallas as pl
from jax.experimental.pallas import tpu as pltpu

F32 = jnp.float32
BF16 = jnp.bfloat16
HIGHEST = lax.Precision.HIGHEST

D_MODEL = 1024
DEPTH = 2
PAST_LEN = 16384
MIX = D_MODEL // 4
N_BRANCH = 4
RWKV_HEAD = 64
RWKV_HEADS = MIX // RWKV_HEAD
RWKV_DECAY_LORA = 32
RWKV_AAA_LORA = 32
RWKV_GATE_LORA = 64
RWKV_PROJ = 3 * MIX + RWKV_DECAY_LORA + RWKV_AAA_LORA + RWKV_GATE_LORA
RWKV_GN_EPS = 64e-5
POOL_WINDOWS = (2, 4, 8, 16)
POOL_HIST = 15
CONV_KERNEL = 31
CONV_HIST = CONV_KERNEL - 1
CONV_LN_EPS = 1e-5
SSM_HEAD_DIM = 64
SSM_HEADS = MIX // SSM_HEAD_DIM
SSM_GROUPS = 2
SSM_STATE = 128
SSM_CONV_KERNEL = 4
SSM_CONV_HIST = SSM_CONV_KERNEL - 1
SSM_CONV_DIM = MIX + 2 * SSM_GROUPS * SSM_STATE
SSM_PROJ = MIX + SSM_CONV_DIM + SSM_HEADS
OFF_POOL = RWKV_PROJ
OFF_CONV = OFF_POOL + MIX
OFF_SSM = OFF_CONV + 2 * MIX
OFF_GATE = OFF_SSM + SSM_PROJ
MOE_GROUPS = 4
MOE_PER_GROUP = 4
MOE_EXPERTS = MOE_GROUPS * MOE_PER_GROUP
MOE_HIDDEN = 256
NORM_EPS = 1e-6

LANES = 128
SUBLANES = 8
VMEM_LIMIT_BYTES = 48 * 1024 * 1024

SSM_IN_PAD = 9 * LANES
MIX_IN_PAD = OFF_SSM + SSM_IN_PAD
ROUTER_PAD = LANES
IN_PROJ = OFF_GATE + N_BRANCH * D_MODEL
GATE_COL0 = OFF_GATE // LANES * LANES
GATE_SHIFT = OFF_GATE - GATE_COL0
GATE_SLAB = -(-(IN_PROJ - GATE_COL0) // LANES) * LANES
GATE_WIN = D_MODEL + LANES
SSD_CHUNK = 128
RWKV_CHUNK = 16
RWKV_BATCH_BLOCK = 8
MOE_EXPERT_BLOCK = MOE_PER_GROUP
NEG_BIG = -1e30


def _sigmoid(x):
    return 1.0 / (1.0 + jnp.exp(-x))


def _silu(x):
    return x * _sigmoid(x)


def _softplus(x):
    return jnp.maximum(x, 0.0) + jnp.log(1.0 + jnp.exp(-jnp.abs(x)))


def _dot(a, b):
    return jnp.dot(a, b, preferred_element_type=F32)


def _dot_hi(a, b):
    return jnp.dot(a, b, preferred_element_type=F32, precision=HIGHEST)


def _dot_nt(a, b, precision=None):
    return lax.dot_general(a, b, (((1,), (1,)), ((), ())), preferred_element_type=F32, precision=precision)


def _eye(n):
    return (lax.broadcasted_iota(jnp.int32, (n, n), 0) == lax.broadcasted_iota(jnp.int32, (n, n), 1)).astype(F32)


def _head_ones(n, head):
    r = lax.broadcasted_iota(jnp.int32, (n, n), 0) // head
    c = lax.broadcasted_iota(jnp.int32, (n, n), 1) // head
    return (r == c).astype(F32)


def _params(sem):
    return pltpu.CompilerParams(dimension_semantics=sem, vmem_limit_bytes=VMEM_LIMIT_BYTES)


def _const_spec(shape):
    nd = len(shape)
    return pl.BlockSpec(shape, lambda *_: (0,) * nd)


def _cast_w_in_kernel(w_ref, mix_ref, gate_ref):
    rows = w_ref.shape[0]
    lane = lax.broadcasted_iota(jnp.int32, (rows, LANES), 1)
    mix_ref[:, 0:GATE_COL0] = w_ref[:, 0:GATE_COL0].astype(BF16)
    tail = jnp.where(lane < GATE_SHIFT, w_ref[:, GATE_COL0:GATE_COL0 + LANES], 0.0)
    mix_ref[:, GATE_COL0:MIX_IN_PAD] = tail.astype(BF16)
    whole = (IN_PROJ - GATE_COL0) // LANES * LANES
    gate_ref[:, 0:whole] = w_ref[:, GATE_COL0:GATE_COL0 + whole].astype(BF16)
    gate_ref[:, whole:GATE_SLAB] = jnp.zeros((rows, GATE_SLAB - whole), BF16)
    gate_ref[:, whole:IN_PROJ - GATE_COL0] = w_ref[:, GATE_COL0 + whole:IN_PROJ].astype(BF16)


def _cast_w_in(w_in, l):
    tr = 128
    return pl.pallas_call(
        _cast_w_in_kernel,
        grid=(D_MODEL // tr,),
        in_specs=[pl.BlockSpec((None, tr, IN_PROJ), lambda i: (l, i, 0))],
        out_specs=[pl.BlockSpec((tr, MIX_IN_PAD), lambda i: (i, 0)), pl.BlockSpec((tr, GATE_SLAB), lambda i: (i, 0))],
        out_shape=[jax.ShapeDtypeStruct((D_MODEL, MIX_IN_PAD), BF16),
                   jax.ShapeDtypeStruct((D_MODEL, GATE_SLAB), BF16)],
        compiler_params=_params(("parallel",)),
        name="cast_w_in",
    )(w_in)


def _inproj_kernel(x_ref, g_ref, w_ref, h_ref, rw_ref, pool_ref, conv_ref, ssm_ref):
    x = x_ref[...]
    y = x * lax.rsqrt(jnp.mean(x * x, -1, keepdims=True) + NORM_EPS) * g_ref[...]
    hb = y.astype(BF16)
    h_ref[...] = hb
    rw_ref[...] = _dot(hb, w_ref[:, 0:OFF_POOL])
    pool_ref[...] = _dot(hb, w_ref[:, OFF_POOL:OFF_CONV])
    conv_ref[...] = _dot(hb, w_ref[:, OFF_CONV:OFF_SSM])
    ssm_ref[...] = _dot(hb, w_ref[:, OFF_SSM:MIX_IN_PAD])


def _inproj(x2, g, w_mix):
    t = x2.shape[0]
    tm = min(t, 512)
    row = lambda w: pl.BlockSpec((tm, w), lambda i: (i, 0))
    return pl.pallas_call(
        _inproj_kernel,
        grid=(t // tm,),
        in_specs=[row(D_MODEL), _const_spec((1, D_MODEL)), _const_spec((D_MODEL, MIX_IN_PAD))],
        out_specs=[row(D_MODEL), row(RWKV_PROJ), row(MIX), row(2 * MIX), row(SSM_IN_PAD)],
        out_shape=[jax.ShapeDtypeStruct((t, D_MODEL), BF16),
                   jax.ShapeDtypeStruct((t, RWKV_PROJ), F32),
                   jax.ShapeDtypeStruct((t, MIX), F32),
                   jax.ShapeDtypeStruct((t, 2 * MIX), F32),
                   jax.ShapeDtypeStruct((t, SSM_IN_PAD), F32)],
        compiler_params=_params(("parallel",)),
        name="inproj",
    )(x2, g, w_mix)


def _rwkv_prep_math(p, prev, mu, w0, a0, kkw, kaw, w2p, a2p, g2p):
    q = p + (prev - p) * mu
    r = q[:, 0:MIX]
    k = q[:, MIX:2 * MIX]
    v = q[:, 2 * MIX:3 * MIX]
    lora = q[:, 3 * MIX:RWKV_PROJ]
    zw = w0 + _dot_hi(jnp.tanh(lora), w2p)
    decay = jnp.exp(-math.exp(-0.5) * _sigmoid(zw))
    a = _sigmoid(a0 + _dot_hi(lora, a2p))
    g = _dot_hi(_sigmoid(lora), g2p)
    kk = k * kkw
    ss = _dot_hi(kk * kk, _head_ones(MIX, RWKV_HEAD))
    kk = kk * lax.rsqrt(jnp.maximum(ss, 1e-24))
    k2 = k * (1.0 + (a - 1.0) * kaw)
    return r, k2, v, kk, kk * a, decay, g


def _rwkv_prep_seq_kernel(p_ref, s0_ref, mu_ref, w0_ref, a0_ref, kkw_ref, kaw_ref, w2_ref, a2_ref, g2_ref,
                          r_ref, k_ref, v_ref, kk_ref, b_ref, d_ref, g_ref, carry_ref):
    @pl.when(pl.program_id(1) == 0)
    def _():
        carry_ref[...] = s0_ref[...]

    p = p_ref[...]
    rows = p.shape[0]
    first = lax.broadcasted_iota(jnp.int32, p.shape, 0) == 0
    prev = jnp.where(first, carry_ref[...], pltpu.roll(p, 1, 0))
    carry_ref[...] = p[rows - 1:rows, :]
    outs = _rwkv_prep_math(p, prev, mu_ref[...], w0_ref[...], a0_ref[...], kkw_ref[...], kaw_ref[...],
                           w2_ref[...], a2_ref[...], g2_ref[...])
    for ref, val in zip((r_ref, k_ref, v_ref, kk_ref, b_ref, d_ref, g_ref), outs):
        ref[...] = val


def _rwkv_prep_tok_kernel(p_ref, prev_ref, mu_ref, w0_ref, a0_ref, kkw_ref, kaw_ref, w2_ref, a2_ref, g2_ref,
                          r_ref, k_ref, v_ref, kk_ref, b_ref, d_ref, g_ref):
    outs = _rwkv_prep_math(p_ref[...], prev_ref[...], mu_ref[...], w0_ref[...], a0_ref[...], kkw_ref[...],
                           kaw_ref[...], w2_ref[...], a2_ref[...], g2_ref[...])
    for ref, val in zip((r_ref, k_ref, v_ref, kk_ref, b_ref, d_ref, g_ref), outs):
        ref[...] = val


def _rwkv_prep(p3, shift0, lw):
    b, l, _ = p3.shape
    consts = [lw['rwkv_mu'], lw['rwkv_w0'], lw['rwkv_a0'], lw['rwkv_kk'], lw['rwkv_ka'],
              lw['rwkv_w2p'], lw['rwkv_a2p'], lw['rwkv_g2p']]
    const_specs = [_const_spec(c.shape) for c in consts]
    if l == 1:
        outs = pl.pallas_call(
            _rwkv_prep_tok_kernel,
            grid=(1,),
            in_specs=[_const_spec((b, RWKV_PROJ)), _const_spec((b, RWKV_PROJ))] + const_specs,
            out_specs=[_const_spec((b, MIX))] * 7,
            out_shape=[jax.ShapeDtypeStruct((b, MIX), F32)] * 7,
            compiler_params=_params(("arbitrary",)),
            name="rwkv_prep_tok",
        )(p3[:, 0], shift0, *consts)
        return [o[:, None, :] for o in outs]
    lt = min(l, 512)
    seq = lambda w: pl.BlockSpec((None, lt, w), lambda i, j: (i, j, 0))
    return pl.pallas_call(
        _rwkv_prep_seq_kernel,
        grid=(b, l // lt),
        in_specs=[seq(RWKV_PROJ), pl.BlockSpec((None, 1, RWKV_PROJ), lambda i, j: (i, 0, 0))] + const_specs,
        out_specs=[seq(MIX)] * 7,
        out_shape=[jax.ShapeDtypeStruct((b, l, MIX), F32)] * 7,
        scratch_shapes=[pltpu.VMEM((1, RWKV_PROJ), F32)],
        compiler_params=_params(("parallel", "arbitrary")),
        name="rwkv_prep_seq",
    )(p3, shift0[:, None, :], *consts)


def _rwkv_scan_kernel(r_ref, k_ref, v_ref, kk_ref, b_ref, d_ref, g_ref, lng_ref, lnb_ref, rk_ref, s0_ref,
                      o_ref, s1_ref, st_ref, col_ref, row_ref, y_ref, *, nb, lc):
    c = pl.program_id(2)
    n = RWKV_HEAD

    @pl.when(c == 0)
    def _():
        for bb in range(nb):
            st_ref[bb] = _dot_nt(_eye(n), s0_ref[bb], HIGHEST)

    ones = _head_ones(LANES, n).astype(BF16)
    ones2 = jnp.concatenate([ones, ones], axis=0)
    lane = lax.broadcasted_iota(jnp.int32, (n, LANES), 1)
    sub = lax.broadcasted_iota(jnp.int32, (n, LANES), 0)
    diag = (lane % n == sub).astype(BF16)
    low = lax.broadcasted_iota(jnp.int32, (lc, LANES), 1) < n

    def head_sum(x):
        s_lo = jnp.sum(jnp.where(low, x, 0.0), -1, keepdims=True)
        s_hi = jnp.sum(jnp.where(low, 0.0, x), -1, keepdims=True)
        return jnp.where(low, s_lo, s_hi)

    def spread2(x):
        hi = x.astype(BF16)
        lo = (x - hi.astype(F32)).astype(BF16)
        parts = [(p[:, None, :] * diag[None, :, :]).reshape(lc * n, LANES) for p in (hi, lo)]
        return jnp.concatenate(parts, axis=-1)

    for bb in range(nb):
        r = r_ref[bb]
        k = k_ref[bb]
        kk = kk_ref[bb]
        bv = b_ref[bb]
        d = d_ref[bb]
        for idx, x in enumerate((kk, d, bv, k, d * r)):
            if idx in (0, 4):
                col_ref[idx, bb] = _dot((x.astype(BF16)[:, None, :] * diag[None, :, :]).reshape(lc * n, LANES), ones)
            else:
                col_ref[idx, bb] = _dot(spread2(x), ones2)
        row_ref[0, bb] = head_sum(bv * r)
        row_ref[1, bb] = head_sum(k * r)

    def step(t, carry):
        base = pl.multiple_of(t * n, n)
        for bb in range(nb):
            st = st_ref[bb]
            ckk = col_ref[0, bb, pl.ds(base, n), :]
            cd = col_ref[1, bb, pl.ds(base, n), :]
            cb = col_ref[2, bb, pl.ds(base, n), :]
            ck = col_ref[3, bb, pl.ds(base, n), :]
            cdr = col_ref[4, bb, pl.ds(base, n), :]
            vrow = v_ref[bb, pl.ds(t, 1), :]
            br = row_ref[0, bb, pl.ds(t, 1), :]
            kr = row_ref[1, bb, pl.ds(t, 1), :]
            sa = -jnp.sum(st * ckk, axis=0, keepdims=True)
            y = jnp.sum(st * cdr, axis=0, keepdims=True) + sa * br + vrow * kr
            st_ref[bb] = st * cd + sa * cb + vrow * ck
            y_ref[bb, pl.ds(t, 1), :] = y
        return carry

    lax.fori_loop(0, lc, step, 0)

    lng = lng_ref[...]
    lnb = lnb_ref[...]
    rk = rk_ref[...]
    inv_n = 1.0 / n
    for bb in range(nb):
        y = y_ref[bb]
        mu = head_sum(y) * inv_n
        yc = y - mu
        var = head_sum(yc * yc) * inv_n
        yn = yc * lax.rsqrt(var + RWKV_GN_EPS) * lng + lnb
        r = r_ref[bb]
        v = v_ref[bb]
        rkv = head_sum(r * k_ref[bb] * rk)
        o_ref[bb] = (yn + rkv * v) * g_ref[bb]

    @pl.when(c == pl.num_programs(2) - 1)
    def _():
        for bb in range(nb):
            s1_ref[bb] = _dot_nt(_eye(LANES), st_ref[bb], HIGHEST)


def _rwkv_scan(prep, lw, s0t):
    r, k, v, kk, bv, d, g = prep
    b, l, _ = r.shape
    nb = RWKV_BATCH_BLOCK
    lc = min(l, RWKV_CHUNK)
    n = RWKV_HEAD
    seq = pl.BlockSpec((nb, lc, LANES), lambda i, p, c: (i, c, p))
    vec = pl.BlockSpec((1, LANES), lambda i, p, c: (0, p))
    st = pl.BlockSpec((nb, None, LANES, n), lambda i, p, c: (i, p, 0, 0))
    kern = functools.partial(_rwkv_scan_kernel, nb=nb, lc=lc)
    return pl.pallas_call(
        kern,
        grid=(b // nb, 2, l // lc),
        in_specs=[seq] * 7 + [vec] * 3 + [st],
        out_specs=[seq, st],
        out_shape=[jax.ShapeDtypeStruct((b, l, MIX), F32), jax.ShapeDtypeStruct((b, 2, LANES, n), F32)],
        scratch_shapes=[pltpu.VMEM((nb, n, LANES), F32),
                        pltpu.VMEM((5, nb, lc * n, LANES), F32),
                        pltpu.VMEM((2, nb, lc, LANES), F32),
                        pltpu.VMEM((nb, lc, LANES), F32)],
        compiler_params=_params(("parallel", "parallel", "arbitrary")),
        name="rwkv_scan",
    )(r, k, v, kk, bv, d, g, lw['rwkv_ln_g'], lw['rwkv_ln_b'], lw['rwkv_rk'], s0t)


def _wkv_to_pairs(s):
    return s.reshape(s.shape[0], 2, 2 * RWKV_HEAD, RWKV_HEAD)


def _wkv_from_pairs(s):
    return s.reshape(s.shape[0], RWKV_HEADS, RWKV_HEAD, RWKV_HEAD)


def _pool_math(tap, pos, pw, scale):
    rows = pos.shape[0]
    lane = lax.broadcasted_iota(jnp.int32, (rows, LANES), 1)
    low = lane < (LANES // 2)
    cnt = lambda w: jnp.minimum(pos + 1, w).astype(F32)
    a0 = tap(0, 0)
    s2 = a0 + tap(1, 0)
    s4 = s2 + tap(2, 0) + tap(3, 0)
    b0 = tap(0, 1)
    s8 = b0
    for k in range(1, 8):
        s8 = s8 + tap(k, 1)
    s16 = s8
    for k in range(8, 16):
        s16 = s16 + tap(k, 1)
    diff_a = jnp.where(low, s2 / cnt(2), s4 / cnt(4)) - a0
    diff_b = jnp.where(low, s8 / cnt(8), s16 / cnt(16)) - b0
    diff = jnp.concatenate([diff_a, diff_b], axis=-1).astype(BF16)
    return _dot(diff, pw) * scale


def _pool_seq_kernel(u_ref, h_ref, pw_ref, sc_ref, o_ref, h1_ref, full_ref, *, l, tt):
    pad = POOL_HIST + 1
    full_ref[0:1, :] = jnp.zeros((1, MIX), F32)
    full_ref[1:pad, :] = h_ref[...]
    full_ref[pad:pad + l, :] = u_ref[...]
    pw = pw_ref[...]
    sc = sc_ref[...]
    for i in range(l // tt):
        t0 = i * tt
        tap = lambda k, half: full_ref[pad + t0 - k:pad + t0 - k + tt, half * LANES:(half + 1) * LANES]
        pos = t0 + lax.broadcasted_iota(jnp.int32, (tt, 1), 0)
        o_ref[t0:t0 + tt, :] = _pool_math(tap, pos, pw, sc)
    h1_ref[...] = full_ref[l + 1:l + pad, :]


def _pool_tok_kernel(u_ref, ht_ref, pw_ref, sc_ref, o_ref, *, pos0):
    def tap(k, half):
        sl = slice(half * LANES, (half + 1) * LANES)
        return u_ref[:, sl] if k == 0 else ht_ref[POOL_HIST - k, :, sl]
    rows = u_ref.shape[0]
    pos = jnp.full((rows, 1), pos0, jnp.int32)
    o_ref[...] = _pool_math(tap, pos, pw_ref[...], sc_ref[...])


def _pool(u3, hist, pos0, lw):
    b, l, _ = u3.shape
    consts = [lw['pool_wbd'], lw['pool_scale']]
    const_specs = [_const_spec(c.shape) for c in consts]
    if l == 1:
        ht = jnp.swapaxes(hist, 0, 1)
        out = pl.pallas_call(
            functools.partial(_pool_tok_kernel, pos0=pos0),
            grid=(1,),
            in_specs=[_const_spec((b, MIX)), _const_spec((POOL_HIST, b, MIX))] + const_specs,
            out_specs=_const_spec((b, MIX)),
            out_shape=jax.ShapeDtypeStruct((b, MIX), F32),
            compiler_params=_params(("arbitrary",)),
            name="pool_tok",
        )(u3[:, 0], ht, *consts)
        return out[:, None, :], jnp.concatenate([hist[:, 1:], u3], axis=1)
    assert pos0 == 0 and l >= POOL_HIST
    tt = min(l, 256)
    seq = pl.BlockSpec((None, l, MIX), lambda i: (i, 0, 0))
    hs = pl.BlockSpec((None, POOL_HIST, MIX), lambda i: (i, 0, 0))
    return pl.pallas_call(
        functools.partial(_pool_seq_kernel, l=l, tt=tt),
        grid=(b,),
        in_specs=[seq, hs] + const_specs,
        out_specs=[seq, hs],
        out_shape=[jax.ShapeDtypeStruct((b, l, MIX), F32), jax.ShapeDtypeStruct((b, POOL_HIST, MIX), F32)],
        scratch_shapes=[pltpu.VMEM((l + POOL_HIST + 1, MIX), F32)],
        compiler_params=_params(("parallel",)),
        name="pool_seq",
    )(u3, hist, *consts)


def _conv_post(y, cb, g, beta):
    y = y + cb
    mu = jnp.mean(y, -1, keepdims=True)
    yc = y - mu
    var = jnp.mean(yc * yc, -1, keepdims=True)
    return _silu(yc * lax.rsqrt(var + CONV_LN_EPS) * g + beta)


def _glu(s):
    return s[:, 0:MIX] * _sigmoid(s[:, MIX:2 * MIX])


def _conv_seq_kernel(s_ref, h_ref, w_ref, cb_ref, g_ref, beta_ref, o_ref, h1_ref, full_ref, *, l, tt):
    pad = CONV_HIST + 2
    full_ref[0:2, :] = jnp.zeros((2, MIX), F32)
    full_ref[2:pad, :] = h_ref[...]
    full_ref[pad:pad + l, :] = _glu(s_ref[...])
    cb = cb_ref[...]
    g = g_ref[...]
    beta = beta_ref[...]

    for i in range(l // tt):
        t0 = i * tt
        acc = jnp.zeros((tt, MIX), F32)
        for k in range(CONV_KERNEL):
            acc = acc + w_ref[k:k + 1, :] * full_ref[t0 + k + 2:t0 + k + 2 + tt, :]
        o_ref[t0:t0 + tt, :] = _conv_post(acc, cb, g, beta)
    h1_ref[...] = full_ref[l + 2:l + pad, :]


def _conv_tok_kernel(s_ref, ht_ref, w_ref, cb_ref, g_ref, beta_ref, o_ref, c_ref):
    c = _glu(s_ref[...])
    c_ref[...] = c
    acc = w_ref[CONV_HIST:CONV_KERNEL, :] * c
    for k in range(CONV_HIST):
        acc = acc + w_ref[k:k + 1, :] * ht_ref[k]
    o_ref[...] = _conv_post(acc, cb_ref[...], g_ref[...], beta_ref[...])


def _conv(s3, hist, lw):
    b, l, _ = s3.shape
    consts = [lw['conv_w'], lw['conv_b'], lw['conv_ln_g'], lw['conv_ln_b']]
    const_specs = [_const_spec(c.shape) for c in consts]
    if l == 1:
        ht = jnp.swapaxes(hist, 0, 1)
        out, c = pl.pallas_call(
            _conv_tok_kernel,
            grid=(1,),
            in_specs=[_const_spec((b, 2 * MIX)), _const_spec((CONV_HIST, b, MIX))] + const_specs,
            out_specs=[_const_spec((b, MIX))] * 2,
            out_shape=[jax.ShapeDtypeStruct((b, MIX), F32)] * 2,
            compiler_params=_params(("arbitrary",)),
            name="conv_tok",
        )(s3[:, 0], ht, *consts)
        return out[:, None, :], jnp.concatenate([hist[:, 1:], c[:, None, :]], axis=1)
    assert l >= CONV_HIST
    tt = min(l, 64)
    hs = pl.BlockSpec((None, CONV_HIST, MIX), lambda i: (i, 0, 0))
    return pl.pallas_call(
        functools.partial(_conv_seq_kernel, l=l, tt=tt),
        grid=(b,),
        in_specs=[pl.BlockSpec((None, l, 2 * MIX), lambda i: (i, 0, 0)), hs] + const_specs,
        out_specs=[pl.BlockSpec((None, l, MIX), lambda i: (i, 0, 0)), hs],
        out_shape=[jax.ShapeDtypeStruct((b, l, MIX), F32), jax.ShapeDtypeStruct((b, CONV_HIST, MIX), F32)],
        scratch_shapes=[pltpu.VMEM((l + CONV_HIST + 2, MIX), F32)],
        compiler_params=_params(("parallel",)),
        name="conv_seq",
    )(s3, hist, *consts)


def _ssd_kernel(zx_ref, ch_ref, h0_ref, cw_ref, cb_ref, dtb_ref, alog_ref, dd_ref, ng_ref,
                o_ref, h1_ref, xf_ref, ht_ref, *, lb, lvalid):
    q = SSD_CHUNK
    c = pl.program_id(1)
    halo = SUBLANES
    gw = SSM_GROUPS * SSM_STATE

    @pl.when(c == 0)
    def _():
        ht_ref[...] = h0_ref[...]
        xf_ref[...] = jnp.zeros(xf_ref.shape, F32)
        xf_ref[halo - SSM_CONV_HIST:halo, :] = ch_ref[...]

    xf_ref[halo:halo + lb, :] = zx_ref[:, MIX:MIX + SSM_CONV_DIM]
    acc = jnp.zeros((q, SSM_CONV_DIM), F32)
    for k in range(SSM_CONV_KERNEL):
        acc = acc + cw_ref[k:k + 1, :] * xf_ref[halo - SSM_CONV_HIST + k:halo - SSM_CONV_HIST + k + q, :]
    xbc = _silu(acc + cb_ref[...])
    xf_ref[halo - SSM_CONV_HIST:halo, :] = xf_ref[halo + q - SSM_CONV_HIST:halo + q, :]

    xs = xbc[:, 0:MIX]
    bm = xbc[:, MIX:MIX + gw]
    cm = xbc[:, MIX + gw:MIX + 2 * gw]

    hsel = (lax.broadcasted_iota(jnp.int32, (LANES, MIX), 0)
            == lax.broadcasted_iota(jnp.int32, (LANES, MIX), 1) // SSM_HEAD_DIM).astype(F32)
    dt_raw = zx_ref[:, MIX + SSM_CONV_DIM:SSM_IN_PAD]
    if lb < q:
        dt_raw = jnp.concatenate([dt_raw, jnp.zeros((q - lb, LANES), F32)], axis=0)
    trow = lax.broadcasted_iota(jnp.int32, (q, 1), 0) + c * q
    dt = jnp.where(trow < lvalid, _softplus(dt_raw + dtb_ref[...]), 0.0)
    dte = _dot_hi(dt, hsel)
    dae = dte * (-jnp.exp(alog_ref[...]))
    ti = lax.broadcasted_iota(jnp.int32, (q, q), 0)
    tj = lax.broadcasted_iota(jnp.int32, (q, q), 1)
    causal = ti >= tj
    acse = _dot_hi(causal.astype(F32), dae)
    xdt = xs * dte
    last = acse[q - 1:q, :]
    w_end = xdt * jnp.exp(last - acse)
    eacs = jnp.exp(acse)

    lane_g = lax.broadcasted_iota(jnp.int32, (q, LANES), 1)
    heads_per_group = SSM_HEADS // SSM_GROUPS
    ys = []
    for g in range(SSM_GROUPS):
        lo = g * LANES
        bg = bm[:, g * SSM_STATE:(g + 1) * SSM_STATE].astype(BF16)
        cg = cm[:, g * SSM_STATE:(g + 1) * SSM_STATE].astype(BF16)
        gmat = _dot_nt(cg, bg)
        hg = ht_ref[lo:lo + LANES, :]
        yg = _dot_nt(cg, hg.astype(BF16)) * eacs[:, lo:lo + LANES]
        xg = xdt[:, lo:lo + LANES]
        for hh in range(heads_per_group):
            col = acse[:, lo + hh * SSM_HEAD_DIM:lo + hh * SSM_HEAD_DIM + 1]
            row = jnp.transpose(jnp.broadcast_to(col, (q, q)))
            seg = jnp.where(causal, col - row, NEG_BIG)
            scores = (gmat * jnp.exp(seg)).astype(BF16)
            xm = jnp.where(lane_g // SSM_HEAD_DIM == hh, xg, 0.0).astype(BF16)
            yg = yg + _dot(scores, xm)
        ys.append(yg)
        st = _dot(jnp.transpose(w_end[:, lo:lo + LANES]).astype(BF16), bg)
        for hh in range(heads_per_group):
            r0 = hh * SSM_HEAD_DIM
            decay = jnp.exp(last[:, lo + r0:lo + r0 + 1])
            ht_ref[lo + r0:lo + r0 + SSM_HEAD_DIM, :] = hg[r0:r0 + SSM_HEAD_DIM] * decay + st[r0:r0 + SSM_HEAD_DIM]
    y = jnp.concatenate(ys, axis=-1)

    y = y + dd_ref[...] * xs
    z = zx_ref[:, 0:MIX]
    if lb < q:
        z = jnp.concatenate([z, jnp.zeros((q - lb, MIX), F32)], axis=0)
    y = y * _silu(z)
    y = y * lax.rsqrt(jnp.mean(y * y, -1, keepdims=True) + NORM_EPS) * ng_ref[...]
    o_ref[...] = y[0:lb, :]

    @pl.when(c == pl.num_programs(1) - 1)
    def _():
        h1_ref[...] = ht_ref[...]


def _ssd(zx3, conv_hist, h0t, lvalid, lw):
    b, lp, _ = zx3.shape
    q = SSD_CHUNK
    lb = min(lp, q)
    consts = [lw['ssm_conv_w'], lw['ssm_conv_b'], lw['ssm_dt_bias_pad'], lw['ssm_a_log_e'], lw['ssm_d_e'],
              lw['ssm_norm_g']]
    const_specs = [pl.BlockSpec(cst.shape, lambda i, c: (0, 0)) for cst in consts]
    hs = pl.BlockSpec((None, MIX, SSM_STATE), lambda i, c: (i, 0, 0))
    return pl.pallas_call(
        functools.partial(_ssd_kernel, lb=lb, lvalid=lvalid),
        grid=(b, lp // lb),
        in_specs=[pl.BlockSpec((None, lb, SSM_IN_PAD), lambda i, c: (i, c, 0)),
                  pl.BlockSpec((None, SSM_CONV_HIST, SSM_CONV_DIM), lambda i, c: (i, 0, 0)), hs] + const_specs,
        out_specs=[pl.BlockSpec((None, lb, MIX), lambda i, c: (i, c, 0)), hs],
        out_shape=[jax.ShapeDtypeStruct((b, lp, MIX), F32), jax.ShapeDtypeStruct((b, MIX, SSM_STATE), F32)],
        scratch_shapes=[pltpu.VMEM((SUBLANES + q, SSM_CONV_DIM), F32), pltpu.VMEM((MIX, SSM_STATE), F32)],
        compiler_params=_params(("parallel", "arbitrary")),
        name="ssd",
    )(zx3, conv_hist, h0t, *consts)


def _ssm_to_rows(s):
    return s.reshape(s.shape[0], MIX, SSM_STATE)


def _ssm_from_rows(s):
    return s.reshape(s.shape[0], SSM_HEADS, SSM_HEAD_DIM, SSM_STATE)


def _merge_kernel(x_ref, h_ref, b0_ref, b1_ref, b2_ref, b3_ref, wg_ref, wb_ref, wo_ref, ng_ref, wr_ref, br_ref,
                  x1_ref, h2_ref, lg_ref):
    h = h_ref[...]
    merged = None
    for kbr, b_ref in enumerate((b0_ref, b1_ref, b2_ref, b3_ref)):
        gate = _sigmoid(_dot(h, wg_ref[:, kbr * D_MODEL:kbr * D_MODEL + GATE_WIN]))
        up = _dot(b_ref[...].astype(BF16), wb_ref[kbr])
        merged = up * gate if merged is None else merged + up * gate
    x1 = x_ref[...] + _dot(merged.astype(BF16), wo_ref[...])
    x1_ref[...] = x1
    hn = x1 * lax.rsqrt(jnp.mean(x1 * x1, -1, keepdims=True) + NORM_EPS) * ng_ref[...]
    h2_ref[...] = hn.astype(BF16)
    lg_ref[...] = _dot_hi(hn, wr_ref[...]) + br_ref[...]


def _merge(x2, h2d, branches, lw):
    t = x2.shape[0]
    tm = min(t, 256)
    row = lambda w: pl.BlockSpec((tm, w), lambda i: (i, 0))
    consts = [lw['w_gate'], lw['w_branch'], lw['w_out'], lw['norm_ffn_g'], lw['router_w'], lw['router_b']]
    return pl.pallas_call(
        _merge_kernel,
        grid=(t // tm,),
        in_specs=[row(D_MODEL), row(D_MODEL)] + [row(MIX)] * 4 + [_const_spec(c.shape) for c in consts],
        out_specs=[row(D_MODEL), row(D_MODEL), row(ROUTER_PAD)],
        out_shape=[jax.ShapeDtypeStruct((t, D_MODEL), F32), jax.ShapeDtypeStruct((t, D_MODEL), BF16),
                   jax.ShapeDtypeStruct((t, ROUTER_PAD), F32)],
        compiler_params=_params(("parallel",)),
        name="merge",
    )(x2, h2d, *branches, *consts)


def _route(logits):
    lane = lax.broadcasted_iota(jnp.int32, logits.shape, 1).astype(F32)
    far = float(4 * LANES)
    is_group = lane < MOE_GROUPS
    gl = jnp.where(is_group, logits, NEG_BIG)
    gmax = jnp.max(gl, -1, keepdims=True)
    gidx = jnp.min(jnp.where(is_group & (gl == gmax), lane, far), -1, keepdims=True)
    gsum = jnp.sum(jnp.where(is_group, jnp.exp(gl - gmax), 0.0), -1, keepdims=True)
    first = MOE_GROUPS + MOE_PER_GROUP * gidx
    in_group = (lane >= first) & (lane < first + MOE_PER_GROUP)
    el = jnp.where(in_group, logits, NEG_BIG)
    m1 = jnp.max(el, -1, keepdims=True)
    i1 = jnp.min(jnp.where(in_group & (el == m1), lane, far), -1, keepdims=True)
    rest = in_group & (lane != i1)
    el2 = jnp.where(rest, logits, NEG_BIG)
    m2 = jnp.max(el2, -1, keepdims=True)
    i2 = jnp.min(jnp.where(rest & (el2 == m2), lane, far), -1, keepdims=True)
    e2 = jnp.exp(m2 - m1)
    w1 = 1.0 / (1.0 + e2)
    w2 = e2 / (1.0 + e2)
    return (jnp.where(lane == i1, w1, 0.0) + jnp.where(lane == i2, w2, 0.0)) / gsum


def _moe_kernel(h_ref, lg_ref, x1_ref, wgu_ref, wd_ref, fg_ref, o_ref, gate_ref, acc_ref, *, final_norm):
    s = pl.program_id(1)
    eb = wgu_ref.shape[0]

    @pl.when(s == 0)
    def _():
        gate_ref[...] = _route(lg_ref[...])
        acc_ref[...] = jnp.zeros(acc_ref.shape, F32)

    lane = lax.broadcasted_iota(jnp.int32, gate_ref.shape, 1)
    gate = gate_ref[...]
    h = h_ref[...]
    acts = []
    for j in range(eb):
        gcol = jnp.sum(jnp.where(lane == s * eb + j + MOE_GROUPS, gate, 0.0), -1, keepdims=True)
        hgu = _dot(h, wgu_ref[j])
        acts.append((_silu(hgu[:, 0:MOE_HIDDEN]) * hgu[:, MOE_HIDDEN:2 * MOE_HIDDEN] * gcol).astype(BF16))
    act = jnp.concatenate(acts, axis=-1)
    acc_ref[...] += _dot(act, wd_ref[...].reshape(eb * MOE_HIDDEN, D_MODEL))

    @pl.when(s == pl.num_programs(1) - 1)
    def _():
        x2 = x1_ref[...] + acc_ref[...]
        if final_norm:
            x2 = x2 * lax.rsqrt(jnp.mean(x2 * x2, -1, keepdims=True) + NORM_EPS) * fg_ref[...]
        o_ref[...] = x2


def _moe(h2d, logits, x1, lw, final_g, final_norm):
    t = x1.shape[0]
    tm = min(t, 512)
    row = lambda w: pl.BlockSpec((tm, w), lambda i, e: (i, 0))
    return pl.pallas_call(
        functools.partial(_moe_kernel, final_norm=final_norm),
        grid=(t // tm, MOE_EXPERTS // MOE_EXPERT_BLOCK),
        in_specs=[row(D_MODEL), row(ROUTER_PAD), row(D_MODEL),
                  pl.BlockSpec((MOE_EXPERT_BLOCK, D_MODEL, 2 * MOE_HIDDEN), lambda i, e: (e, 0, 0)),
                  pl.BlockSpec((MOE_EXPERT_BLOCK, MOE_HIDDEN, D_MODEL), lambda i, e: (e, 0, 0)),
                  pl.BlockSpec((1, D_MODEL), lambda i, e: (0, 0))],
        out_specs=row(D_MODEL),
        out_shape=jax.ShapeDtypeStruct((t, D_MODEL), F32),
        scratch_shapes=[pltpu.VMEM((tm, ROUTER_PAD), F32), pltpu.VMEM((tm, D_MODEL), F32)],
        compiler_params=_params(("parallel", "arbitrary")),
        name="moe",
    )(h2d, logits, x1, lw['moe_w_gu'], lw['moe_w_down'], final_g)


def _prep_layer(params, l):
    g = lambda name: params[name][l]
    row = lambda a: a.reshape(1, -1).astype(F32)
    lora_rows = RWKV_PROJ - 3 * MIX

    def lora_pad(w, start):
        return jnp.zeros((lora_rows, MIX), F32).at[start:start + w.shape[0]].set(w)

    pw = g('pool_w')
    pool_wbd = jnp.zeros((MIX, MIX), F32)
    gw = MIX // len(POOL_WINDOWS)
    for gi in range(len(POOL_WINDOWS)):
        pool_wbd = pool_wbd.at[gi * gw:(gi + 1) * gw, gi * gw:(gi + 1) * gw].set(pw[gi])
    router_w = jnp.concatenate([g('moe_router_group_w'), g('moe_router_expert_w')], axis=1)
    router_b = jnp.concatenate([g('moe_router_group_b'), g('moe_router_expert_b')])
    n_r = router_w.shape[1]
    w_mix, w_gate = _cast_w_in(params['w_in'], l)
    lane_pad = (GATE_SHIFT, LANES - GATE_SHIFT)
    return {
        'norm_mix_g': row(g('norm_mix_g')),
        'w_mix': w_mix,
        'w_gate': w_gate,
        'rwkv_mu': row(g('rwkv_mu')), 'rwkv_w0': row(g('rwkv_w0')), 'rwkv_a0': row(g('rwkv_a0')),
        'rwkv_kk': row(g('rwkv_kk')), 'rwkv_ka': row(g('rwkv_ka')),
        'rwkv_w2p': lora_pad(g('rwkv_w2'), 0),
        'rwkv_a2p': lora_pad(g('rwkv_a2'), RWKV_DECAY_LORA),
        'rwkv_g2p': lora_pad(g('rwkv_g2'), RWKV_DECAY_LORA + RWKV_AAA_LORA),
        'rwkv_rk': row(g('rwkv_rk')), 'rwkv_ln_g': row(g('rwkv_ln_g')), 'rwkv_ln_b': row(g('rwkv_ln_b')),
        'pool_wbd': pool_wbd.astype(BF16), 'pool_scale': row(g('pool_scale')),
        'conv_w': g('conv_w'), 'conv_b': row(g('conv_b')),
        'conv_ln_g': row(g('conv_ln_g')), 'conv_ln_b': row(g('conv_ln_b')),
        'ssm_conv_w': g('ssm_conv_w'), 'ssm_conv_b': row(g('ssm_conv_b')),
        'ssm_dt_bias_pad': jnp.pad(row(g('ssm_dt_bias')), ((0, 0), (0, LANES - SSM_HEADS))),
        'ssm_a_log_e': row(jnp.repeat(g('ssm_a_log'), SSM_HEAD_DIM)),
        'ssm_d_e': row(jnp.repeat(g('ssm_d'), SSM_HEAD_DIM)),
        'ssm_norm_g': row(g('ssm_norm_g')),
        'w_branch': jnp.pad(g('w_branch'), ((0, 0), (0, 0), lane_pad)).astype(BF16),
        'w_out': jnp.pad(g('w_out'), (lane_pad, (0, 0))).astype(BF16),
        'norm_ffn_g': row(g('norm_ffn_g')),
        'router_w': jnp.pad(router_w, ((0, 0), (0, ROUTER_PAD - n_r))),
        'router_b': jnp.pad(row(router_b), ((0, 0), (0, ROUTER_PAD - n_r))),
        'moe_w_gu': jnp.concatenate([g('moe_w_gate'), g('moe_w_up')], axis=-1).astype(BF16),
        'moe_w_down': g('moe_w_down').astype(BF16),
    }


def _layer(x2, st, b, l, pos0, lw, final_g, final_norm):
    shift0, wkv0, pool0, conv0, sconv0, ssm0 = st
    h, p_rwkv, u_pool, s_conv, zx = _inproj(x2, lw['norm_mix_g'], lw['w_mix'])
    seq = lambda a: a.reshape(b, l, a.shape[-1])

    p3 = seq(p_rwkv)
    o_rwkv, wkv1t = _rwkv_scan(_rwkv_prep(p3, shift0, lw), lw, _wkv_to_pairs(wkv0))
    o_pool, pool1 = _pool(seq(u_pool), pool0, pos0, lw)
    o_conv, conv1 = _conv(seq(s_conv), conv0, lw)
    zx3 = seq(zx)
    xbc_raw = zx3[:, :, MIX:MIX + SSM_CONV_DIM]
    if l >= SSM_CONV_HIST:
        sconv1 = xbc_raw[:, l - SSM_CONV_HIST:]
    else:
        sconv1 = jnp.concatenate([sconv0, xbc_raw], axis=1)[:, -SSM_CONV_HIST:]
    lpad = -l % SUBLANES
    zx3p = jnp.pad(zx3, ((0, 0), (0, lpad), (0, 0))) if lpad else zx3
    o_ssm, ssm1t = _ssd(zx3p, sconv0, _ssm_to_rows(ssm0), l, lw)
    o_ssm = o_ssm[:, :l]

    flat = lambda a: a.reshape(b * l, MIX)
    x1, h2, logits = _merge(x2, h, [flat(o_rwkv), flat(o_pool), flat(o_conv), flat(o_ssm)], lw)
    x_out = _moe(h2, logits, x1, lw, final_g, final_norm)
    return x_out, (p3[:, -1], _wkv_from_pairs(wkv1t), pool1, conv1, sconv1, _ssm_from_rows(ssm1t))


def _trunk(x, states, pos0, layers, final_g):
    b, l, _ = x.shape
    x2 = x.reshape(b * l, D_MODEL)
    new = [[] for _ in states]
    for li, lw in enumerate(layers):
        st = tuple(s[li] for s in states)
        x2, st1 = _layer(x2, st, b, l, pos0, lw, final_g, li == len(layers) - 1)
        for lst, s in zip(new, st1):
            lst.append(s)
    return x2.reshape(b, l, D_MODEL), tuple(jnp.stack(lst) for lst in new)


def kernel(x_prompt, x_sample, state_rwkv_shift, state_rwkv_wkv, state_pool, state_conv, state_ssm_conv, state_ssm,
           norm_mix_g, w_in, rwkv_mu, rwkv_w0, rwkv_w2, rwkv_a0, rwkv_a2, rwkv_g2, rwkv_kk, rwkv_ka, rwkv_rk,
           rwkv_ln_g, rwkv_ln_b, pool_w, pool_scale, conv_w, conv_b, conv_ln_g, conv_ln_b, ssm_conv_w, ssm_conv_b,
           ssm_dt_bias, ssm_a_log, ssm_d, ssm_norm_g, w_branch, w_out, norm_ffn_g, moe_router_group_w,
           moe_router_group_b, moe_router_expert_w, moe_router_expert_b, moe_w_gate, moe_w_up, moe_w_down,
           final_norm_g):
    params = {
        'norm_mix_g': norm_mix_g, 'w_in': w_in, 'rwkv_mu': rwkv_mu, 'rwkv_w0': rwkv_w0, 'rwkv_w2': rwkv_w2,
        'rwkv_a0': rwkv_a0, 'rwkv_a2': rwkv_a2, 'rwkv_g2': rwkv_g2, 'rwkv_kk': rwkv_kk, 'rwkv_ka': rwkv_ka,
        'rwkv_rk': rwkv_rk, 'rwkv_ln_g': rwkv_ln_g, 'rwkv_ln_b': rwkv_ln_b, 'pool_w': pool_w,
        'pool_scale': pool_scale, 'conv_w': conv_w, 'conv_b': conv_b, 'conv_ln_g': conv_ln_g,
        'conv_ln_b': conv_ln_b, 'ssm_conv_w': ssm_conv_w, 'ssm_conv_b': ssm_conv_b, 'ssm_dt_bias': ssm_dt_bias,
        'ssm_a_log': ssm_a_log, 'ssm_d': ssm_d, 'ssm_norm_g': ssm_norm_g, 'w_branch': w_branch, 'w_out': w_out,
        'norm_ffn_g': norm_ffn_g, 'moe_router_group_w': moe_router_group_w,
        'moe_router_group_b': moe_router_group_b, 'moe_router_expert_w': moe_router_expert_w,
        'moe_router_expert_b': moe_router_expert_b, 'moe_w_gate': moe_w_gate, 'moe_w_up': moe_w_up,
        'moe_w_down': moe_w_down,
    }
    layers = [_prep_layer(params, l) for l in range(DEPTH)]
    final_g = final_norm_g.reshape(1, D_MODEL).astype(F32)
    bp, dtp = x_prompt.shape[0], x_prompt.dtype
    empty = (jnp.zeros((DEPTH, bp, RWKV_PROJ), dtp),
             jnp.zeros((DEPTH, bp, RWKV_HEADS, RWKV_HEAD, RWKV_HEAD), dtp),
             jnp.zeros((DEPTH, bp, POOL_HIST, MIX), dtp),
             jnp.zeros((DEPTH, bp, CONV_HIST, MIX), dtp),
             jnp.zeros((DEPTH, bp, SSM_CONV_HIST, SSM_CONV_DIM), dtp),
             jnp.zeros((DEPTH, bp, SSM_HEADS, SSM_HEAD_DIM, SSM_STATE), dtp))
    y_prompt, p_states = _trunk(x_prompt, empty, 0, layers, final_g)
    past = (state_rwkv_shift, state_rwkv_wkv, state_pool, state_conv, state_ssm_conv, state_ssm)
    y_sample, s_states = _trunk(x_sample, past, PAST_LEN, layers, final_g)
    return (y_prompt, y_sample) + tuple(p_states) + tuple(s_states)
```

```python
import functools
import math

import jax
import jax.numpy as jnp
from jax import lax
from jax.experimental import pallas as pl
from jax.experimental.pallas import tpu as pltpu

F32 = jnp.float32
BF16 = jnp.bfloat16
HIGHEST = lax.Precision.HIGHEST

D_MODEL = 1024
DEPTH = 2
PAST_LEN = 16384
MIX = D_MODEL // 4
N_BRANCH = 4
RWKV_HEAD = 64
RWKV_HEADS = MIX // RWKV_HEAD
RWKV_DECAY_LORA = 32
RWKV_AAA_LORA = 32
RWKV_GATE_LORA = 64
RWKV_PROJ = 3 * MIX + RWKV_DECAY_LORA + RWKV_AAA_LORA + RWKV_GATE_LORA
RWKV_GN_EPS = 64e-5
POOL_WINDOWS = (2, 4, 8, 16)
POOL_HIST = 15
CONV_KERNEL = 31
CONV_HIST = CONV_KERNEL - 1
CONV_LN_EPS = 1e-5
SSM_HEAD_DIM = 64
SSM_HEADS = MIX // SSM_HEAD_DIM
SSM_GROUPS = 2
SSM_STATE = 128
SSM_CONV_KERNEL = 4
SSM_CONV_HIST = SSM_CONV_KERNEL - 1
SSM_CONV_DIM = MIX + 2 * SSM_GROUPS * SSM_STATE
SSM_PROJ = MIX + SSM_CONV_DIM + SSM_HEADS
OFF_POOL = RWKV_PROJ
OFF_CONV = OFF_POOL + MIX
OFF_SSM = OFF_CONV + 2 * MIX
OFF_GATE = OFF_SSM + SSM_PROJ
MOE_GROUPS = 4
MOE_PER_GROUP = 4
MOE_EXPERTS = MOE_GROUPS * MOE_PER_GROUP
MOE_HIDDEN = 256
NORM_EPS = 1e-6

LANES = 128
SUBLANES = 8
VMEM_LIMIT_BYTES = 48 * 1024 * 1024

SSM_IN_PAD = 9 * LANES
MIX_IN_PAD = OFF_SSM + SSM_IN_PAD
ROUTER_PAD = LANES
IN_PROJ = OFF_GATE + N_BRANCH * D_MODEL
GATE_COL0 = OFF_GATE // LANES * LANES
GATE_SHIFT = OFF_GATE - GATE_COL0
GATE_SLAB = -(-(IN_PROJ - GATE_COL0) // LANES) * LANES
GATE_WIN = D_MODEL + LANES
SSD_CHUNK = 128
RWKV_CHUNK = 16
RWKV_BATCH_BLOCK = 8
MOE_EXPERT_BLOCK = MOE_PER_GROUP
NEG_BIG = -1e30


def _sigmoid(x):
    return 1.0 / (1.0 + jnp.exp(-x))


def _silu(x):
    return x * _sigmoid(x)


def _softplus(x):
    return jnp.maximum(x, 0.0) + jnp.log(1.0 + jnp.exp(-jnp.abs(x)))


def _dot(a, b):
    return jnp.dot(a, b, preferred_element_type=F32)


def _dot_hi(a, b):
    return jnp.dot(a, b, preferred_element_type=F32, precision=HIGHEST)


def _dot_nt(a, b, precision=None):
    return lax.dot_general(a, b, (((1,), (1,)), ((), ())), preferred_element_type=F32, precision=precision)


def _eye(n):
    return (lax.broadcasted_iota(jnp.int32, (n, n), 0) == lax.broadcasted_iota(jnp.int32, (n, n), 1)).astype(F32)


def _head_ones(n, head):
    r = lax.broadcasted_iota(jnp.int32, (n, n), 0) // head
    c = lax.broadcasted_iota(jnp.int32, (n, n), 1) // head
    return (r == c).astype(F32)


def _params(sem):
    return pltpu.CompilerParams(dimension_semantics=sem, vmem_limit_bytes=VMEM_LIMIT_BYTES)


def _const_spec(shape):
    nd = len(shape)
    return pl.BlockSpec(shape, lambda *_: (0,) * nd)


def _cast_w_in_kernel(w_ref, mix_ref, gate_ref):
    rows = w_ref.shape[0]
    lane = lax.broadcasted_iota(jnp.int32, (rows, LANES), 1)
    mix_ref[:, 0:GATE_COL0] = w_ref[:, 0:GATE_COL0].astype(BF16)
    tail = jnp.where(lane < GATE_SHIFT, w_ref[:, GATE_COL0:GATE_COL0 + LANES], 0.0)
    mix_ref[:, GATE_COL0:MIX_IN_PAD] = tail.astype(BF16)
    whole = (IN_PROJ - GATE_COL0) // LANES * LANES
    gate_ref[:, 0:whole] = w_ref[:, GATE_COL0:GATE_COL0 + whole].astype(BF16)
    gate_ref[:, whole:GATE_SLAB] = jnp.zeros((rows, GATE_SLAB - whole), BF16)
    gate_ref[:, whole:IN_PROJ - GATE_COL0] = w_ref[:, GATE_COL0 + whole:IN_PROJ].astype(BF16)


def _cast_w_in(w_in, l):
    tr = 128
    return pl.pallas_call(
        _cast_w_in_kernel,
        grid=(D_MODEL // tr,),
        in_specs=[pl.BlockSpec((None, tr, IN_PROJ), lambda i: (l, i, 0))],
        out_specs=[pl.BlockSpec((tr, MIX_IN_PAD), lambda i: (i, 0)), pl.BlockSpec((tr, GATE_SLAB), lambda i: (i, 0))],
        out_shape=[jax.ShapeDtypeStruct((D_MODEL, MIX_IN_PAD), BF16),
                   jax.ShapeDtypeStruct((D_MODEL, GATE_SLAB), BF16)],
        compiler_params=_params(("parallel",)),
        name="cast_w_in",
    )(w_in)


def _inproj_kernel(x_ref, g_ref, w_ref, h_ref, rw_ref, pool_ref, conv_ref, ssm_ref):
    x = x_ref[...]
    y = x * lax.rsqrt(jnp.mean(x * x, -1, keepdims=True) + NORM_EPS) * g_ref[...]
    hb = y.astype(BF16)
    h_ref[...] = hb
    rw_ref[...] = _dot(hb, w_ref[:, 0:OFF_POOL])
    pool_ref[...] = _dot(hb, w_ref[:, OFF_POOL:OFF_CONV])
    conv_ref[...] = _dot(hb, w_ref[:, OFF_CONV:OFF_SSM])
    ssm_ref[...] = _dot(hb, w_ref[:, OFF_SSM:MIX_IN_PAD])


def _inproj(x2, g, w_mix):
    t = x2.shape[0]
    tm = min(t, 512)
    row = lambda w: pl.BlockSpec((tm, w), lambda i: (i, 0))
    return pl.pallas_call(
        _inproj_kernel,
        grid=(t // tm,),
        in_specs=[row(D_MODEL), _const_spec((1, D_MODEL)), _const_spec((D_MODEL, MIX_IN_PAD))],
        out_specs=[row(D_MODEL), row(RWKV_PROJ), row(MIX), row(2 * MIX), row(SSM_IN_PAD)],
        out_shape=[jax.ShapeDtypeStruct((t, D_MODEL), BF16),
                   jax.ShapeDtypeStruct((t, RWKV_PROJ), F32),
                   jax.ShapeDtypeStruct((t, MIX), F32),
                   jax.ShapeDtypeStruct((t, 2 * MIX), F32),
                   jax.ShapeDtypeStruct((t, SSM_IN_PAD), F32)],
        compiler_params=_params(("parallel",)),
        name="inproj",
    )(x2, g, w_mix)


def _rwkv_prep_math(p, prev, mu, w0, a0, kkw, kaw, w2p, a2p, g2p):
    q = p + (prev - p) * mu
    r = q[:, 0:MIX]
    k = q[:, MIX:2 * MIX]
    v = q[:, 2 * MIX:3 * MIX]
    lora = q[:, 3 * MIX:RWKV_PROJ]
    zw = w0 + _dot_hi(jnp.tanh(lora), w2p)
    decay = jnp.exp(-math.exp(-0.5) * _sigmoid(zw))
    a = _sigmoid(a0 + _dot_hi(lora, a2p))
    g = _dot_hi(_sigmoid(lora), g2p)
    kk = k * kkw
    ss = _dot_hi(kk * kk, _head_ones(MIX, RWKV_HEAD))
    kk = kk * lax.rsqrt(jnp.maximum(ss, 1e-24))
    k2 = k * (1.0 + (a - 1.0) * kaw)
    return r, k2, v, kk, kk * a, decay, g


def _rwkv_prep_seq_kernel(p_ref, s0_ref, mu_ref, w0_ref, a0_ref, kkw_ref, kaw_ref, w2_ref, a2_ref, g2_ref,
                          r_ref, k_ref, v_ref, kk_ref, b_ref, d_ref, g_ref, carry_ref):
    @pl.when(pl.program_id(1) == 0)
    def _():
        carry_ref[...] = s0_ref[...]

    p = p_ref[...]
    rows = p.shape[0]
    first = lax.broadcasted_iota(jnp.int32, p.shape, 0) == 0
    prev = jnp.where(first, carry_ref[...], pltpu.roll(p, 1, 0))
    carry_ref[...] = p[rows - 1:rows, :]
    outs = _rwkv_prep_math(p, prev, mu_ref[...], w0_ref[...], a0_ref[...], kkw_ref[...], kaw_ref[...],
                           w2_ref[...], a2_ref[...], g2_ref[...])
    for ref, val in zip((r_ref, k_ref, v_ref, kk_ref, b_ref, d_ref, g_ref), outs):
        ref[...] = val


def _rwkv_prep_tok_kernel(p_ref, prev_ref, mu_ref, w0_ref, a0_ref, kkw_ref, kaw_ref, w2_ref, a2_ref, g2_ref,
                          r_ref, k_ref, v_ref, kk_ref, b_ref, d_ref, g_ref):
    outs = _rwkv_prep_math(p_ref[...], prev_ref[...], mu_ref[...], w0_ref[...], a0_ref[...], kkw_ref[...],
                           kaw_ref[...], w2_ref[...], a2_ref[...], g2_ref[...])
    for ref, val in zip((r_ref, k_ref, v_ref, kk_ref, b_ref, d_ref, g_ref), outs):
        ref[...] = val


def _rwkv_prep(p3, shift0, lw):
    b, l, _ = p3.shape
    consts = [lw['rwkv_mu'], lw['rwkv_w0'], lw['rwkv_a0'], lw['rwkv_kk'], lw['rwkv_ka'],
              lw['rwkv_w2p'], lw['rwkv_a2p'], lw['rwkv_g2p']]
    const_specs = [_const_spec(c.shape) for c in consts]
    if l == 1:
        outs = pl.pallas_call(
            _rwkv_prep_tok_kernel,
            grid=(1,),
            in_specs=[_const_spec((b, RWKV_PROJ)), _const_spec((b, RWKV_PROJ))] + const_specs,
            out_specs=[_const_spec((b, MIX))] * 7,
            out_shape=[jax.ShapeDtypeStruct((b, MIX), F32)] * 7,
            compiler_params=_params(("arbitrary",)),
            name="rwkv_prep_tok",
        )(p3[:, 0], shift0, *consts)
        return [o[:, None, :] for o in outs]
    lt = min(l, 512)
    seq = lambda w: pl.BlockSpec((None, lt, w), lambda i, j: (i, j, 0))
    return pl.pallas_call(
        _rwkv_prep_seq_kernel,
        grid=(b, l // lt),
        in_specs=[seq(RWKV_PROJ), pl.BlockSpec((None, 1, RWKV_PROJ), lambda i, j: (i, 0, 0))] + const_specs,
        out_specs=[seq(MIX)] * 7,
        out_shape=[jax.ShapeDtypeStruct((b, l, MIX), F32)] * 7,
        scratch_shapes=[pltpu.VMEM((1, RWKV_PROJ), F32)],
        compiler_params=_params(("parallel", "arbitrary")),
        name="rwkv_prep_seq",
    )(p3, shift0[:, None, :], *consts)


def _rwkv_scan_kernel(r_ref, k_ref, v_ref, kk_ref, b_ref, d_ref, g_ref, lng_ref, lnb_ref, rk_ref, s0_ref,
                      o_ref, s1_ref, st_ref, col_ref, row_ref, y_ref, *, nb, lc):
    c = pl.program_id(2)
    n = RWKV_HEAD

    @pl.when(c == 0)
    def _():
        for bb in range(nb):
            st_ref[bb] = _dot_nt(_eye(n), s0_ref[bb], HIGHEST)

    ones = _head_ones(LANES, n).astype(BF16)
    ones2 = jnp.concatenate([ones, ones], axis=0)
    lane = lax.broadcasted_iota(jnp.int32, (n, LANES), 1)
    sub = lax.broadcasted_iota(jnp.int32, (n, LANES), 0)
    diag = (lane % n == sub).astype(BF16)
    low = lax.broadcasted_iota(jnp.int32, (lc, LANES), 1) < n

    def head_sum(x):
        s_lo = jnp.sum(jnp.where(low, x, 0.0), -1, keepdims=True)
        s_hi = jnp.sum(jnp.where(low, 0.0, x), -1, keepdims=True)
        return jnp.where(low, s_lo, s_hi)

    def spread2(x):
        hi = x.astype(BF16)
        lo = (x - hi.astype(F32)).astype(BF16)
        rows = x.shape[0]
        parts = [(p[:, None, :] * diag[None, :, :]).reshape(rows * n, LANES) for p in (hi, lo)]
        return jnp.concatenate(parts, axis=-1)

    trow = lax.broadcasted_iota(jnp.int32, (lc, LANES), 0)
    for bb in range(nb):
        r = r_ref[bb]
        k = k_ref[bb]
        kk = kk_ref[bb]
        bv = b_ref[bb]
        logd = jnp.log(d_ref[bb])
        lp = logd
        shift = 1
        while shift < lc:
            lp = lp + jnp.where(trow >= shift, pltpu.roll(lp, shift, 0), 0.0)
            shift *= 2
        p_in = jnp.exp(lp)
        inv = jnp.exp(-lp)
        for idx, x in ((0, kk * jnp.exp(lp - logd)), (2, bv * inv), (3, k * inv), (4, r * p_in)):
            if idx in (0, 4):
                col_ref[idx, bb] = _dot((x.astype(BF16)[:, None, :] * diag[None, :, :]).reshape(lc * n, LANES), ones)
            else:
                col_ref[idx, bb] = _dot(spread2(x), ones2)
        col_ref[1, bb, 0:n, :] = _dot(spread2(p_in[lc - 1:lc, :]), ones2)
        row_ref[0, bb] = head_sum(bv * r)
        row_ref[1, bb] = head_sum(k * r)

    def step(t, carry):
        base = pl.multiple_of(t * n, n)
        for bb in range(nb):
            st = st_ref[bb]
            ckk = col_ref[0, bb, pl.ds(base, n), :]
            cb = col_ref[2, bb, pl.ds(base, n), :]
            ck = col_ref[3, bb, pl.ds(base, n), :]
            cdr = col_ref[4, bb, pl.ds(base, n), :]
            vrow = v_ref[bb, pl.ds(t, 1), :]
            br = row_ref[0, bb, pl.ds(t, 1), :]
            kr = row_ref[1, bb, pl.ds(t, 1), :]
            sa = -jnp.sum(st * ckk, axis=0, keepdims=True)
            y = jnp.sum(st * cdr, axis=0, keepdims=True) + sa * br + vrow * kr
            st_ref[bb] = st + sa * cb + vrow * ck
            y_ref[bb, pl.ds(t, 1), :] = y
        return carry

    lax.fori_loop(0, lc, step, 0)
    for bb in range(nb):
        st_ref[bb] = st_ref[bb] * col_ref[1, bb, 0:n, :]

    lng = lng_ref[...]
    lnb = lnb_ref[...]
    rk = rk_ref[...]
    inv_n = 1.0 / n
    for bb in range(nb):
        y = y_ref[bb]
        mu = head_sum(y) * inv_n
        yc = y - mu
        var = head_sum(yc * yc) * inv_n
        yn = yc * lax.rsqrt(var + RWKV_GN_EPS) * lng + lnb
        r = r_ref[bb]
        v = v_ref[bb]
        rkv = head_sum(r * k_ref[bb] * rk)
        o_ref[bb] = (yn + rkv * v) * g_ref[bb]

    @pl.when(c == pl.num_programs(2) - 1)
    def _():
        for bb in range(nb):
            s1_ref[bb] = _dot_nt(_eye(LANES), st_ref[bb], HIGHEST)


def _rwkv_scan(prep, lw, s0t):
    r, k, v, kk, bv, d, g = prep
    b, l, _ = r.shape
    nb = RWKV_BATCH_BLOCK
    lc = min(l, RWKV_CHUNK)
    n = RWKV_HEAD
    seq = pl.BlockSpec((nb, lc, LANES), lambda i, p, c: (i, c, p))
    vec = pl.BlockSpec((1, LANES), lambda i, p, c: (0, p))
    st = pl.BlockSpec((nb, None, LANES, n), lambda i, p, c: (i, p, 0, 0))
    kern = functools.partial(_rwkv_scan_kernel, nb=nb, lc=lc)
    return pl.pallas_call(
        kern,
        grid=(b // nb, 2, l // lc),
        in_specs=[seq] * 7 + [vec] * 3 + [st],
        out_specs=[seq, st],
        out_shape=[jax.ShapeDtypeStruct((b, l, MIX), F32), jax.ShapeDtypeStruct((b, 2, LANES, n), F32)],
        scratch_shapes=[pltpu.VMEM((nb, n, LANES), F32),
                        pltpu.VMEM((5, nb, lc * n, LANES), F32),
                        pltpu.VMEM((2, nb, lc, LANES), F32),
                        pltpu.VMEM((nb, lc, LANES), F32)],
        compiler_params=_params(("parallel", "parallel", "arbitrary")),
        name="rwkv_scan",
    )(r, k, v, kk, bv, d, g, lw['rwkv_ln_g'], lw['rwkv_ln_b'], lw['rwkv_rk'], s0t)


def _wkv_to_pairs(s):
    return s.reshape(s.shape[0], 2, 2 * RWKV_HEAD, RWKV_HEAD)


def _wkv_from_pairs(s):
    return s.reshape(s.shape[0], RWKV_HEADS, RWKV_HEAD, RWKV_HEAD)


def _pool_math(tap, pos, pw, scale):
    rows = pos.shape[0]
    lane = lax.broadcasted_iota(jnp.int32, (rows, LANES), 1)
    low = lane < (LANES // 2)
    cnt = lambda w: jnp.minimum(pos + 1, w).astype(F32)
    a0 = tap(0, 0)
    s2 = a0 + tap(1, 0)
    s4 = s2 + tap(2, 0) + tap(3, 0)
    b0 = tap(0, 1)
    s8 = b0
    for k in range(1, 8):
        s8 = s8 + tap(k, 1)
    s16 = s8
    for k in range(8, 16):
        s16 = s16 + tap(k, 1)
    diff_a = jnp.where(low, s2 / cnt(2), s4 / cnt(4)) - a0
    diff_b = jnp.where(low, s8 / cnt(8), s16 / cnt(16)) - b0
    diff = jnp.concatenate([diff_a, diff_b], axis=-1).astype(BF16)
    return _dot(diff, pw) * scale


def _pool_seq_kernel(u_ref, h_ref, pw_ref, sc_ref, o_ref, h1_ref, full_ref, *, l, tt):
    pad = POOL_HIST + 1
    full_ref[0:1, :] = jnp.zeros((1, MIX), F32)
    full_ref[1:pad, :] = h_ref[...]
    full_ref[pad:pad + l, :] = u_ref[...]
    pw = pw_ref[...]
    sc = sc_ref[...]
    for i in range(l // tt):
        t0 = i * tt
        tap = lambda k, half: full_ref[pad + t0 - k:pad + t0 - k + tt, half * LANES:(half + 1) * LANES]
        pos = t0 + lax.broadcasted_iota(jnp.int32, (tt, 1), 0)
        o_ref[t0:t0 + tt, :] = _pool_math(tap, pos, pw, sc)
    h1_ref[...] = full_ref[l + 1:l + pad, :]


def _pool_tok_kernel(u_ref, ht_ref, pw_ref, sc_ref, o_ref, *, pos0):
    def tap(k, half):
        sl = slice(half * LANES, (half + 1) * LANES)
        return u_ref[:, sl] if k == 0 else ht_ref[POOL_HIST - k, :, sl]
    rows = u_ref.shape[0]
    pos = jnp.full((rows, 1), pos0, jnp.int32)
    o_ref[...] = _pool_math(tap, pos, pw_ref[...], sc_ref[...])


def _pool(u3, hist, pos0, lw):
    b, l, _ = u3.shape
    consts = [lw['pool_wbd'], lw['pool_scale']]
    const_specs = [_const_spec(c.shape) for c in consts]
    if l == 1:
        ht = jnp.swapaxes(hist, 0, 1)
        out = pl.pallas_call(
            functools.partial(_pool_tok_kernel, pos0=pos0),
            grid=(1,),
            in_specs=[_const_spec((b, MIX)), _const_spec((POOL_HIST, b, MIX))] + const_specs,
            out_specs=_const_spec((b, MIX)),
            out_shape=jax.ShapeDtypeStruct((b, MIX), F32),
            compiler_params=_params(("arbitrary",)),
            name="pool_tok",
        )(u3[:, 0], ht, *consts)
        return out[:, None, :], jnp.concatenate([hist[:, 1:], u3], axis=1)
    assert pos0 == 0 and l >= POOL_HIST
    tt = min(l, 256)
    seq = pl.BlockSpec((None, l, MIX), lambda i: (i, 0, 0))
    hs = pl.BlockSpec((None, POOL_HIST, MIX), lambda i: (i, 0, 0))
    return pl.pallas_call(
        functools.partial(_pool_seq_kernel, l=l, tt=tt),
        grid=(b,),
        in_specs=[seq, hs] + const_specs,
        out_specs=[seq, hs],
        out_shape=[jax.ShapeDtypeStruct((b, l, MIX), F32), jax.ShapeDtypeStruct((b, POOL_HIST, MIX), F32)],
        scratch_shapes=[pltpu.VMEM((l + POOL_HIST + 1, MIX), F32)],
        compiler_params=_params(("parallel",)),
        name="pool_seq",
    )(u3, hist, *consts)


def _conv_post(y, cb, g, beta):
    y = y + cb
    mu = jnp.mean(y, -1, keepdims=True)
    yc = y - mu
    var = jnp.mean(yc * yc, -1, keepdims=True)
    return _silu(yc * lax.rsqrt(var + CONV_LN_EPS) * g + beta)


def _glu(s):
    return s[:, 0:MIX] * _sigmoid(s[:, MIX:2 * MIX])


def _conv_seq_kernel(s_ref, h_ref, w_ref, cb_ref, g_ref, beta_ref, o_ref, h1_ref, full_ref, *, l, tt):
    pad = CONV_HIST + 2
    full_ref[0:2, :] = jnp.zeros((2, MIX), F32)
    full_ref[2:pad, :] = h_ref[...]
    full_ref[pad:pad + l, :] = _glu(s_ref[...])
    cb = cb_ref[...]
    g = g_ref[...]
    beta = beta_ref[...]

    for i in range(l // tt):
        t0 = i * tt
        acc = jnp.zeros((tt, MIX), F32)
        for k in range(CONV_KERNEL):
            acc = acc + w_ref[k:k + 1, :] * full_ref[t0 + k + 2:t0 + k + 2 + tt, :]
        o_ref[t0:t0 + tt, :] = _conv_post(acc, cb, g, beta)
    h1_ref[...] = full_ref[l + 2:l + pad, :]


def _conv_tok_kernel(s_ref, ht_ref, w_ref, cb_ref, g_ref, beta_ref, o_ref, c_ref):
    c = _glu(s_ref[...])
    c_ref[...] = c
    acc = w_ref[CONV_HIST:CONV_KERNEL, :] * c
    for k in range(CONV_HIST):
        acc = acc + w_ref[k:k + 1, :] * ht_ref[k]
    o_ref[...] = _conv_post(acc, cb_ref[...], g_ref[...], beta_ref[...])


def _conv(s3, hist, lw):
    b, l, _ = s3.shape
    consts = [lw['conv_w'], lw['conv_b'], lw['conv_ln_g'], lw['conv_ln_b']]
    const_specs = [_const_spec(c.shape) for c in consts]
    if l == 1:
        ht = jnp.swapaxes(hist, 0, 1)
        out, c = pl.pallas_call(
            _conv_tok_kernel,
            grid=(1,),
            in_specs=[_const_spec((b, 2 * MIX)), _const_spec((CONV_HIST, b, MIX))] + const_specs,
            out_specs=[_const_spec((b, MIX))] * 2,
            out_shape=[jax.ShapeDtypeStruct((b, MIX), F32)] * 2,
            compiler_params=_params(("arbitrary",)),
            name="conv_tok",
        )(s3[:, 0], ht, *consts)
        return out[:, None, :], jnp.concatenate([hist[:, 1:], c[:, None, :]], axis=1)
    assert l >= CONV_HIST
    tt = min(l, 64)
    hs = pl.BlockSpec((None, CONV_HIST, MIX), lambda i: (i, 0, 0))
    return pl.pallas_call(
        functools.partial(_conv_seq_kernel, l=l, tt=tt),
        grid=(b,),
        in_specs=[pl.BlockSpec((None, l, 2 * MIX), lambda i: (i, 0, 0)), hs] + const_specs,
        out_specs=[pl.BlockSpec((None, l, MIX), lambda i: (i, 0, 0)), hs],
        out_shape=[jax.ShapeDtypeStruct((b, l, MIX), F32), jax.ShapeDtypeStruct((b, CONV_HIST, MIX), F32)],
        scratch_shapes=[pltpu.VMEM((l + CONV_HIST + 2, MIX), F32)],
        compiler_params=_params(("parallel",)),
        name="conv_seq",
    )(s3, hist, *consts)


def _ssd_kernel(zx_ref, ch_ref, h0_ref, cw_ref, cb_ref, dtb_ref, alog_ref, dd_ref, ng_ref,
                o_ref, h1_ref, xf_ref, ht_ref, *, lb, lvalid):
    q = SSD_CHUNK
    c = pl.program_id(1)
    halo = SUBLANES
    gw = SSM_GROUPS * SSM_STATE

    @pl.when(c == 0)
    def _():
        ht_ref[...] = h0_ref[...]
        xf_ref[...] = jnp.zeros(xf_ref.shape, F32)
        xf_ref[halo - SSM_CONV_HIST:halo, :] = ch_ref[...]

    xf_ref[halo:halo + lb, :] = zx_ref[:, MIX:MIX + SSM_CONV_DIM]
    acc = jnp.zeros((q, SSM_CONV_DIM), F32)
    for k in range(SSM_CONV_KERNEL):
        acc = acc + cw_ref[k:k + 1, :] * xf_ref[halo - SSM_CONV_HIST + k:halo - SSM_CONV_HIST + k + q, :]
    xbc = _silu(acc + cb_ref[...])
    xf_ref[halo - SSM_CONV_HIST:halo, :] = xf_ref[halo + q - SSM_CONV_HIST:halo + q, :]

    xs = xbc[:, 0:MIX]
    bm = xbc[:, MIX:MIX + gw]
    cm = xbc[:, MIX + gw:MIX + 2 * gw]

    hsel = (lax.broadcasted_iota(jnp.int32, (LANES, MIX), 0)
            == lax.broadcasted_iota(jnp.int32, (LANES, MIX), 1) // SSM_HEAD_DIM).astype(F32)
    dt_raw = zx_ref[:, MIX + SSM_CONV_DIM:SSM_IN_PAD]
    if lb < q:
        dt_raw = jnp.concatenate([dt_raw, jnp.zeros((q - lb, LANES), F32)], axis=0)
    trow = lax.broadcasted_iota(jnp.int32, (q, 1), 0) + c * q
    dt = jnp.where(trow < lvalid, _softplus(dt_raw + dtb_ref[...]), 0.0)
    dte = _dot_hi(dt, hsel)
    dae = dte * (-jnp.exp(alog_ref[...]))
    ti = lax.broadcasted_iota(jnp.int32, (q, q), 0)
    tj = lax.broadcasted_iota(jnp.int32, (q, q), 1)
    causal = ti >= tj
    acse = _dot_hi(causal.astype(F32), dae)
    xdt = xs * dte
    last = acse[q - 1:q, :]
    w_end = xdt * jnp.exp(last - acse)
    eacs = jnp.exp(acse)

    lane_g = lax.broadcasted_iota(jnp.int32, (q, LANES), 1)
    heads_per_group = SSM_HEADS // SSM_GROUPS
    ys = []
    for g in range(SSM_GROUPS):
        lo = g * LANES
        bg = bm[:, g * SSM_STATE:(g + 1) * SSM_STATE].astype(BF16)
        cg = cm[:, g * SSM_STATE:(g + 1) * SSM_STATE].astype(BF16)
        gmat = _dot_nt(cg, bg)
        hg = ht_ref[lo:lo + LANES, :]
        yg = _dot_nt(cg, hg.astype(BF16)) * eacs[:, lo:lo + LANES]
        xg = xdt[:, lo:lo + LANES]
        for hh in range(heads_per_group):
            col = acse[:, lo + hh * SSM_HEAD_DIM:lo + hh * SSM_HEAD_DIM + 1]
            row = jnp.transpose(jnp.broadcast_to(col, (q, q)))
            seg = jnp.where(causal, col - row, NEG_BIG)
            scores = (gmat * jnp.exp(seg)).astype(BF16)
            xm = jnp.where(lane_g // SSM_HEAD_DIM == hh, xg, 0.0).astype(BF16)
            yg = yg + _dot(scores, xm)
        ys.append(yg)
        st = _dot(jnp.transpose(w_end[:, lo:lo + LANES]).astype(BF16), bg)
        for hh in range(heads_per_group):
            r0 = hh * SSM_HEAD_DIM
            decay = jnp.exp(last[:, lo + r0:lo + r0 + 1])
            ht_ref[lo + r0:lo + r0 + SSM_HEAD_DIM, :] = hg[r0:r0 + SSM_HEAD_DIM] * decay + st[r0:r0 + SSM_HEAD_DIM]
    y = jnp.concatenate(ys, axis=-1)

    y = y + dd_ref[...] * xs
    z = zx_ref[:, 0:MIX]
    if lb < q:
        z = jnp.concatenate([z, jnp.zeros((q - lb, MIX), F32)], axis=0)
    y = y * _silu(z)
    y = y * lax.rsqrt(jnp.mean(y * y, -1, keepdims=True) + NORM_EPS) * ng_ref[...]
    o_ref[...] = y[0:lb, :]

    @pl.when(c == pl.num_programs(1) - 1)
    def _():
        h1_ref[...] = ht_ref[...]


def _ssd(zx3, conv_hist, h0t, lvalid, lw):
    b, lp, _ = zx3.shape
    q = SSD_CHUNK
    lb = min(lp, q)
    consts = [lw['ssm_conv_w'], lw['ssm_conv_b'], lw['ssm_dt_bias_pad'], lw['ssm_a_log_e'], lw['ssm_d_e'],
              lw['ssm_norm_g']]
    const_specs = [pl.BlockSpec(cst.shape, lambda i, c: (0, 0)) for cst in consts]
    hs = pl.BlockSpec((None, MIX, SSM_STATE), lambda i, c: (i, 0, 0))
    return pl.pallas_call(
        functools.partial(_ssd_kernel, lb=lb, lvalid=lvalid),
        grid=(b, lp // lb),
        in_specs=[pl.BlockSpec((None, lb, SSM_IN_PAD), lambda i, c: (i, c, 0)),
                  pl.BlockSpec((None, SSM_CONV_HIST, SSM_CONV_DIM), lambda i, c: (i, 0, 0)), hs] + const_specs,
        out_specs=[pl.BlockSpec((None, lb, MIX), lambda i, c: (i, c, 0)), hs],
        out_shape=[jax.ShapeDtypeStruct((b, lp, MIX), F32), jax.ShapeDtypeStruct((b, MIX, SSM_STATE), F32)],
        scratch_shapes=[pltpu.VMEM((SUBLANES + q, SSM_CONV_DIM), F32), pltpu.VMEM((MIX, SSM_STATE), F32)],
        compiler_params=_params(("parallel", "arbitrary")),
        name="ssd",
    )(zx3, conv_hist, h0t, *consts)


def _ssm_to_rows(s):
    return s.reshape(s.shape[0], MIX, SSM_STATE)


def _ssm_from_rows(s):
    return s.reshape(s.shape[0], SSM_HEADS, SSM_HEAD_DIM, SSM_STATE)


def _merge_kernel(x_ref, h_ref, b0_ref, b1_ref, b2_ref, b3_ref, wg_ref, wb_ref, wo_ref, ng_ref, wr_ref, br_ref,
                  x1_ref, h2_ref, lg_ref):
    h = h_ref[...]
    merged = None
    for kbr, b_ref in enumerate((b0_ref, b1_ref, b2_ref, b3_ref)):
        gate = _sigmoid(_dot(h, wg_ref[:, kbr * D_MODEL:kbr * D_MODEL + GATE_WIN]))
        up = _dot(b_ref[...].astype(BF16), wb_ref[kbr])
        merged = up * gate if merged is None else merged + up * gate
    x1 = x_ref[...] + _dot(merged.astype(BF16), wo_ref[...])
    x1_ref[...] = x1
    hn = x1 * lax.rsqrt(jnp.mean(x1 * x1, -1, keepdims=True) + NORM_EPS) * ng_ref[...]
    h2_ref[...] = hn.astype(BF16)
    lg_ref[...] = _dot_hi(hn, wr_ref[...]) + br_ref[...]


def _merge(x2, h2d, branches, lw):
    t = x2.shape[0]
    tm = min(t, 256)
    row = lambda w: pl.BlockSpec((tm, w), lambda i: (i, 0))
    consts = [lw['w_gate'], lw['w_branch'], lw['w_out'], lw['norm_ffn_g'], lw['router_w'], lw['router_b']]
    return pl.pallas_call(
        _merge_kernel,
        grid=(t // tm,),
        in_specs=[row(D_MODEL), row(D_MODEL)] + [row(MIX)] * 4 + [_const_spec(c.shape) for c in consts],
        out_specs=[row(D_MODEL), row(D_MODEL), row(ROUTER_PAD)],
        out_shape=[jax.ShapeDtypeStruct((t, D_MODEL), F32), jax.ShapeDtypeStruct((t, D_MODEL), BF16),
                   jax.ShapeDtypeStruct((t, ROUTER_PAD), F32)],
        compiler_params=_params(("parallel",)),
        name="merge",
    )(x2, h2d, *branches, *consts)


def _route(logits):
    lane = lax.broadcasted_iota(jnp.int32, logits.shape, 1).astype(F32)
    far = float(4 * LANES)
    is_group = lane < MOE_GROUPS
    gl = jnp.where(is_group, logits, NEG_BIG)
    gmax = jnp.max(gl, -1, keepdims=True)
    gidx = jnp.min(jnp.where(is_group & (gl == gmax), lane, far), -1, keepdims=True)
    gsum = jnp.sum(jnp.where(is_group, jnp.exp(gl - gmax), 0.0), -1, keepdims=True)
    first = MOE_GROUPS + MOE_PER_GROUP * gidx
    in_group = (lane >= first) & (lane < first + MOE_PER_GROUP)
    el = jnp.where(in_group, logits, NEG_BIG)
    m1 = jnp.max(el, -1, keepdims=True)
    i1 = jnp.min(jnp.where(in_group & (el == m1), lane, far), -1, keepdims=True)
    rest = in_group & (lane != i1)
    el2 = jnp.where(rest, logits, NEG_BIG)
    m2 = jnp.max(el2, -1, keepdims=True)
    i2 = jnp.min(jnp.where(rest & (el2 == m2), lane, far), -1, keepdims=True)
    e2 = jnp.exp(m2 - m1)
    w1 = 1.0 / (1.0 + e2)
    w2 = e2 / (1.0 + e2)
    return (jnp.where(lane == i1, w1, 0.0) + jnp.where(lane == i2, w2, 0.0)) / gsum


def _moe_kernel(h_ref, lg_ref, x1_ref, wgu_ref, wd_ref, fg_ref, o_ref, gate_ref, acc_ref, *, final_norm):
    s = pl.program_id(1)
    eb = wgu_ref.shape[0]

    @pl.when(s == 0)
    def _():
        gate_ref[...] = _route(lg_ref[...])
        acc_ref[...] = jnp.zeros(acc_ref.shape, F32)

    lane = lax.broadcasted_iota(jnp.int32, gate_ref.shape, 1)
    gate = gate_ref[...]
    h = h_ref[...]
    acts = []
    for j in range(eb):
        gcol = jnp.sum(jnp.where(lane == s * eb + j + MOE_GROUPS, gate, 0.0), -1, keepdims=True)
        hgu = _dot(h, wgu_ref[j])
        acts.append((_silu(hgu[:, 0:MOE_HIDDEN]) * hgu[:, MOE_HIDDEN:2 * MOE_HIDDEN] * gcol).astype(BF16))
    act = jnp.concatenate(acts, axis=-1)
    acc_ref[...] += _dot(act, wd_ref[...].reshape(eb * MOE_HIDDEN, D_MODEL))

    @pl.when(s == pl.num_programs(1) - 1)
    def _():
        x2 = x1_ref[...] + acc_ref[...]
        if final_norm:
            x2 = x2 * lax.rsqrt(jnp.mean(x2 * x2, -1, keepdims=True) + NORM_EPS) * fg_ref[...]
        o_ref[...] = x2


def _moe(h2d, logits, x1, lw, final_g, final_norm):
    t = x1.shape[0]
    tm = min(t, 512)
    row = lambda w: pl.BlockSpec((tm, w), lambda i, e: (i, 0))
    return pl.pallas_call(
        functools.partial(_moe_kernel, final_norm=final_norm),
        grid=(t // tm, MOE_EXPERTS // MOE_EXPERT_BLOCK),
        in_specs=[row(D_MODEL), row(ROUTER_PAD), row(D_MODEL),
                  pl.BlockSpec((MOE_EXPERT_BLOCK, D_MODEL, 2 * MOE_HIDDEN), lambda i, e: (e, 0, 0)),
                  pl.BlockSpec((MOE_EXPERT_BLOCK, MOE_HIDDEN, D_MODEL), lambda i, e: (e, 0, 0)),
                  pl.BlockSpec((1, D_MODEL), lambda i, e: (0, 0))],
        out_specs=row(D_MODEL),
        out_shape=jax.ShapeDtypeStruct((t, D_MODEL), F32),
        scratch_shapes=[pltpu.VMEM((tm, ROUTER_PAD), F32), pltpu.VMEM((tm, D_MODEL), F32)],
        compiler_params=_params(("parallel", "arbitrary")),
        name="moe",
    )(h2d, logits, x1, lw['moe_w_gu'], lw['moe_w_down'], final_g)


def _prep_layer(params, l):
    g = lambda name: params[name][l]
    row = lambda a: a.reshape(1, -1).astype(F32)
    lora_rows = RWKV_PROJ - 3 * MIX

    def lora_pad(w, start):
        return jnp.zeros((lora_rows, MIX), F32).at[start:start + w.shape[0]].set(w)

    pw = g('pool_w')
    pool_wbd = jnp.zeros((MIX, MIX), F32)
    gw = MIX // len(POOL_WINDOWS)
    for gi in range(len(POOL_WINDOWS)):
        pool_wbd = pool_wbd.at[gi * gw:(gi + 1) * gw, gi * gw:(gi + 1) * gw].set(pw[gi])
    router_w = jnp.concatenate([g('moe_router_group_w'), g('moe_router_expert_w')], axis=1)
    router_b = jnp.concatenate([g('moe_router_group_b'), g('moe_router_expert_b')])
    n_r = router_w.shape[1]
    w_mix, w_gate = _cast_w_in(params['w_in'], l)
    lane_pad = (GATE_SHIFT, LANES - GATE_SHIFT)
    return {
        'norm_mix_g': row(g('norm_mix_g')),
        'w_mix': w_mix,
        'w_gate': w_gate,
        'rwkv_mu': row(g('rwkv_mu')), 'rwkv_w0': row(g('rwkv_w0')), 'rwkv_a0': row(g('rwkv_a0')),
        'rwkv_kk': row(g('rwkv_kk')), 'rwkv_ka': row(g('rwkv_ka')),
        'rwkv_w2p': lora_pad(g('rwkv_w2'), 0),
        'rwkv_a2p': lora_pad(g('rwkv_a2'), RWKV_DECAY_LORA),
        'rwkv_g2p': lora_pad(g('rwkv_g2'), RWKV_DECAY_LORA + RWKV_AAA_LORA),
        'rwkv_rk': row(g('rwkv_rk')), 'rwkv_ln_g': row(g('rwkv_ln_g')), 'rwkv_ln_b': row(g('rwkv_ln_b')),
        'pool_wbd': pool_wbd.astype(BF16), 'pool_scale': row(g('pool_scale')),
        'conv_w': g('conv_w'), 'conv_b': row(g('conv_b')),
        'conv_ln_g': row(g('conv_ln_g')), 'conv_ln_b': row(g('conv_ln_b')),
        'ssm_conv_w': g('ssm_conv_w'), 'ssm_conv_b': row(g('ssm_conv_b')),
        'ssm_dt_bias_pad': jnp.pad(row(g('ssm_dt_bias')), ((0, 0), (0, LANES - SSM_HEADS))),
        'ssm_a_log_e': row(jnp.repeat(g('ssm_a_log'), SSM_HEAD_DIM)),
        'ssm_d_e': row(jnp.repeat(g('ssm_d'), SSM_HEAD_DIM)),
        'ssm_norm_g': row(g('ssm_norm_g')),
        'w_branch': jnp.pad(g('w_branch'), ((0, 0), (0, 0), lane_pad)).astype(BF16),
        'w_out': jnp.pad(g('w_out'), (lane_pad, (0, 0))).astype(BF16),
        'norm_ffn_g': row(g('norm_ffn_g')),
        'router_w': jnp.pad(router_w, ((0, 0), (0, ROUTER_PAD - n_r))),
        'router_b': jnp.pad(row(router_b), ((0, 0), (0, ROUTER_PAD - n_r))),
        'moe_w_gu': jnp.concatenate([g('moe_w_gate'), g('moe_w_up')], axis=-1).astype(BF16),
        'moe_w_down': g('moe_w_down').astype(BF16),
    }


def _layer(x2, st, b, l, pos0, lw, final_g, final_norm):
    shift0, wkv0, pool0, conv0, sconv0, ssm0 = st
    h, p_rwkv, u_pool, s_conv, zx = _inproj(x2, lw['norm_mix_g'], lw['w_mix'])
    seq = lambda a: a.reshape(b, l, a.shape[-1])

    p3 = seq(p_rwkv)
    o_rwkv, wkv1t = _rwkv_scan(_rwkv_prep(p3, shift0, lw), lw, _wkv_to_pairs(wkv0))
    o_pool, pool1 = _pool(seq(u_pool), pool0, pos0, lw)
    o_conv, conv1 = _conv(seq(s_conv), conv0, lw)
    zx3 = seq(zx)
    xbc_raw = zx3[:, :, MIX:MIX + SSM_CONV_DIM]
    if l >= SSM_CONV_HIST:
        sconv1 = xbc_raw[:, l - SSM_CONV_HIST:]
    else:
        sconv1 = jnp.concatenate([sconv0, xbc_raw], axis=1)[:, -SSM_CONV_HIST:]
    lpad = -l % SUBLANES
    zx3p = jnp.pad(zx3, ((0, 0), (0, lpad), (0, 0))) if lpad else zx3
    o_ssm, ssm1t = _ssd(zx3p, sconv0, _ssm_to_rows(ssm0), l, lw)
    o_ssm = o_ssm[:, :l]

    flat = lambda a: a.reshape(b * l, MIX)
    x1, h2, logits = _merge(x2, h, [flat(o_rwkv), flat(o_pool), flat(o_conv), flat(o_ssm)], lw)
    x_out = _moe(h2, logits, x1, lw, final_g, final_norm)
    return x_out, (p3[:, -1], _wkv_from_pairs(wkv1t), pool1, conv1, sconv1, _ssm_from_rows(ssm1t))


def _trunk(x, states, pos0, layers, final_g):
    b, l, _ = x.shape
    x2 = x.reshape(b * l, D_MODEL)
    new = [[] for _ in states]
    for li, lw in enumerate(layers):
        st = tuple(s[li] for s in states)
        x2, st1 = _layer(x2, st, b, l, pos0, lw, final_g, li == len(layers) - 1)
        for lst, s in zip(new, st1):
            lst.append(s)
    return x2.reshape(b, l, D_MODEL), tuple(jnp.stack(lst) for lst in new)


def kernel(x_prompt, x_sample, state_rwkv_shift, state_rwkv_wkv, state_pool, state_conv, state_ssm_conv, state_ssm,
           norm_mix_g, w_in, rwkv_mu, rwkv_w0, rwkv_w2, rwkv_a0, rwkv_a2, rwkv_g2, rwkv_kk, rwkv_ka, rwkv_rk,
           rwkv_ln_g, rwkv_ln_b, pool_w, pool_scale, conv_w, conv_b, conv_ln_g, conv_ln_b, ssm_conv_w, ssm_conv_b,
           ssm_dt_bias, ssm_a_log, ssm_d, ssm_norm_g, w_branch, w_out, norm_ffn_g, moe_router_group_w,
           moe_router_group_b, moe_router_expert_w, moe_router_expert_b, moe_w_gate, moe_w_up, moe_w_down,
           final_norm_g):
    params = {
        'norm_mix_g': norm_mix_g, 'w_in': w_in, 'rwkv_mu': rwkv_mu, 'rwkv_w0': rwkv_w0, 'rwkv_w2': rwkv_w2,
        'rwkv_a0': rwkv_a0, 'rwkv_a2': rwkv_a2, 'rwkv_g2': rwkv_g2, 'rwkv_kk': rwkv_kk, 'rwkv_ka': rwkv_ka,
        'rwkv_rk': rwkv_rk, 'rwkv_ln_g': rwkv_ln_g, 'rwkv_ln_b': rwkv_ln_b, 'pool_w': pool_w,
        'pool_scale': pool_scale, 'conv_w': conv_w, 'conv_b': conv_b, 'conv_ln_g': conv_ln_g,
        'conv_ln_b': conv_ln_b, 'ssm_conv_w': ssm_conv_w, 'ssm_conv_b': ssm_conv_b, 'ssm_dt_bias': ssm_dt_bias,
        'ssm_a_log': ssm_a_log, 'ssm_d': ssm_d, 'ssm_norm_g': ssm_norm_g, 'w_branch': w_branch, 'w_out': w_out,
        'norm_ffn_g': norm_ffn_g, 'moe_router_group_w': moe_router_group_w,
        'moe_router_group_b': moe_router_group_b, 'moe_router_expert_w': moe_router_expert_w,
        'moe_router_expert_b': moe_router_expert_b, 'moe_w_gate': moe_w_gate, 'moe_w_up': moe_w_up,
        'moe_w_down': moe_w_down,
    }
    layers = [_prep_layer(params, l) for l in range(DEPTH)]
    final_g = final_norm_g.reshape(1, D_MODEL).astype(F32)
    bp, dtp = x_prompt.shape[0], x_prompt.dtype
    empty = (jnp.zeros((DEPTH, bp, RWKV_PROJ), dtp),
             jnp.zeros((DEPTH, bp, RWKV_HEADS, RWKV_HEAD, RWKV_HEAD), dtp),
             jnp.zeros((DEPTH, bp, POOL_HIST, MIX), dtp),
             jnp.zeros((DEPTH, bp, CONV_HIST, MIX), dtp),
             jnp.zeros((DEPTH, bp, SSM_CONV_HIST, SSM_CONV_DIM), dtp),
             jnp.zeros((DEPTH, bp, SSM_HEADS, SSM_HEAD_DIM, SSM_STATE), dtp))
    y_prompt, p_states = _trunk(x_prompt, empty, 0, layers, final_g)
    past = (state_rwkv_shift, state_rwkv_wkv, state_pool, state_conv, state_ssm_conv, state_ssm)
    y_sample, s_states = _trunk(x_sample, past, PAST_LEN, layers, final_g)
    return (y_prompt, y_sample) + tuple(p_states) + tuple(s_states)
```
